```python
import math
import jax, jax.numpy as jnp
from jax import lax
import numpy as np

D_MODEL = 1024
BATCH = 32
SEQ = 256
DEPTH = 2
DEC_BATCH = 8
DEC_SEQ = 4096
PAST_LEN = 256

GRID_W = 64
D_MIX = D_MODEL
W_SSM = D_MIX // 4
W_SWA = D_MIX // 4
W_AX = D_MIX // 4
W_FNET = D_MIX - W_SSM - W_SWA - W_AX
HEAD_DIM = 64
N_Q_HEADS = W_SWA // HEAD_DIM
N_KV_HEADS = 2
Q_PER_KV = N_Q_HEADS // N_KV_HEADS
KV_W = N_KV_HEADS * HEAD_DIM
SSM_GROUP = 16
N_SSM_GROUPS = W_SSM // SSM_GROUP
SSM_STATE = 64
WINDOW = 128
Q_BLOCK = 128
ROPE_BASE = 10000.0
D_FF = 2816
N_MOD = 9
EPS = 1e-6
P_IN = W_SSM + W_SWA + 2 * KV_W + W_AX + 2 * KV_W + W_FNET

kernel_name = "hybrid_diffusion_parallel_heads_step"


def rmsnorm(x, g):
    xf = x.astype(jnp.float32)
    y = xf * lax.rsqrt(jnp.mean(xf * xf, axis=-1, keepdims=True) + EPS)
    return (y * g.astype(jnp.float32)).astype(x.dtype)


def swiglu(x, wg, wu, wd):
    return (jax.nn.silu(x @ wg) * (x @ wu)) @ wd


def modulation(cond, w_mod, b_mod):
    m = jax.nn.silu(cond) @ w_mod + b_mod
    return m.reshape(cond.shape[0], 1, N_MOD, D_MODEL)


def axial_rope_tables(n_tokens):
    rows = n_tokens // GRID_W
    row_id = jnp.repeat(jnp.arange(rows), GRID_W).astype(jnp.float32)
    col_id = jnp.tile(jnp.arange(GRID_W), rows).astype(jnp.float32)
    n_freq = HEAD_DIM // 4
    inv = ROPE_BASE ** (-jnp.arange(n_freq, dtype=jnp.float32) / n_freq)
    ang = jnp.concatenate([row_id[:, None] * inv, col_id[:, None] * inv], axis=-1)
    return jnp.cos(ang), jnp.sin(ang)


def apply_rope(x, cos, sin):
    half = HEAD_DIM // 2
    bshape = (cos.shape[0],) + (1,) * (x.ndim - 3) + (half,)
    c = cos.reshape(bshape).astype(x.dtype)
    s = sin.reshape(bshape).astype(x.dtype)
    x1, x2 = x[..., :half], x[..., half:]
    return jnp.concatenate([x1 * c - x2 * s, x2 * c + x1 * s], axis=-1)


def sweep_attention(q, k, v, sink=None, band=False, q_ctx=None, k_ctx=None, v_ctx=None):
    B, L = q.shape[0], q.shape[1]
    scale = HEAD_DIM ** -0.5
    if band:
        pad = ((0, 0), (Q_BLOCK, Q_BLOCK), (0, 0), (0, 0))
        k_src, v_src = jnp.pad(k, pad), jnp.pad(v, pad)

    def one_block(i):
        start = i * Q_BLOCK
        qb = lax.dynamic_slice_in_dim(q, start, Q_BLOCK, axis=1)
        if band:
            kb = lax.dynamic_slice_in_dim(k_src, start, 3 * Q_BLOCK, axis=1)
            vb = lax.dynamic_slice_in_dim(v_src, start, 3 * Q_BLOCK, axis=1)
        else:
            kb, vb = k, v
        logits = jnp.einsum('bqkgd,bskd->bkgqs', qb, kb).astype(jnp.float32) * scale
        if band:
            qpos = start + jnp.arange(Q_BLOCK)
            kpos = start - Q_BLOCK + jnp.arange(3 * Q_BLOCK)
            ok = (kpos[None, :] >= 0) & (kpos[None, :] < L) & (jnp.abs(qpos[:, None] - kpos[None, :]) <= WINDOW)
            logits = jnp.where(ok, logits, -jnp.inf)
        n_main = logits.shape[-1]
        parts = [logits]
        if q_ctx is not None:
            qc = lax.dynamic_slice_in_dim(q_ctx, start, Q_BLOCK, axis=1)
            parts.append(jnp.einsum('bqkgd,bskd->bkgqs', qc, k_ctx).astype(jnp.float32) * scale)
        if sink is not None:
            parts.append(jnp.broadcast_to(sink.astype(jnp.float32)[None, :, :, None, None], logits.shape[:-1] + (1,)))
        probs = jax.nn.softmax(jnp.concatenate(parts, axis=-1), axis=-1)
        out = jnp.einsum('bkgqs,bskd->bqkgd', probs[..., :n_main].astype(vb.dtype), vb)
        if q_ctx is not None:
            n_ctx = k_ctx.shape[1]
            out = out + jnp.einsum('bkgqs,bskd->bqkgd', probs[..., n_main:n_main + n_ctx].astype(v_ctx.dtype), v_ctx)
        return out

    blocks = lax.map(one_block, jnp.arange(L // Q_BLOCK))
    return jnp.moveaxis(blocks, 0, 1).reshape(B, L, q.shape[2], q.shape[3], HEAD_DIM)


def _linear_combine(e1, e2):
    a1, b1 = e1
    a2, b2 = e2
    return a1 * a2, a2 * b1 + b2


def ssm_mixer(u, lam_re, lam_im, b_re, b_im, c_re, c_im, log_dt, d_skip, w_glu, b_glu, s0=None):
    B, L, _ = u.shape
    uf = u.astype(jnp.float32).reshape(B, L, N_SSM_GROUPS, SSM_GROUP)
    lam = lax.complex(lam_re.astype(jnp.float32), lam_im.astype(jnp.float32))
    dt = jnp.exp(log_dt.astype(jnp.float32))[..., None]
    a = jnp.exp(lam * dt)
    b_bar = ((a - 1.0) / lam)[..., None] * lax.complex(b_re.astype(jnp.float32), b_im.astype(jnp.float32))
    c_re32, c_im32 = c_re.astype(jnp.float32), c_im.astype(jnp.float32)
    y = d_skip.astype(jnp.float32) * uf.reshape(B, L, W_SSM)
    finals = []
    for direction in range(2):
        bu = lax.complex(jnp.einsum('blgc,gpc->blgp', uf, jnp.real(b_bar[direction])),
                         jnp.einsum('blgc,gpc->blgp', uf, jnp.imag(b_bar[direction])))
        ad = a[direction]
        edge = 0 if direction == 0 else L - 1
        if s0 is not None:
            bu = bu.at[:, edge].add(ad * s0[:, direction])
        _, states = lax.associative_scan(_linear_combine, (jnp.broadcast_to(ad, bu.shape), bu),
                                         axis=1, reverse=(direction == 1))
        if s0 is None:
            finals.append(states[:, L - 1 - edge])
        y = y + (jnp.einsum('blgp,gcp->blgc', jnp.real(states), c_re32[direction])
                 - jnp.einsum('blgp,gcp->blgc', jnp.imag(states), c_im32[direction])).reshape(B, L, W_SSM)
    y = jax.nn.gelu(y)
    out = (y * jax.nn.sigmoid(y @ w_glu.astype(jnp.float32) + b_glu.astype(jnp.float32))).astype(u.dtype)
    if s0 is None:
        return out, jnp.stack(finals, axis=1)
    return out, None


def fourier_mixer(u, w, b):
    f = jnp.real(jnp.fft.fft2(u.astype(jnp.float32), axes=(1, 2), norm="ortho"))
    return f.astype(u.dtype) @ w + b


def token_mixing(x, p, ctx):
    B, L, _ = x.shape
    proj = x @ p["w_in"]
    sizes = [W_SSM, W_SWA, KV_W, KV_W, W_AX, KV_W, KV_W, W_FNET]
    points, acc = [], 0
    for s in sizes[:-1]:
        acc += s
        points.append(acc)
    u_ssm, q_s, k_s, v_s, q_a, k_a, v_a, u_f = jnp.split(proj, points, axis=-1)
    q_s = q_s.reshape(B, L, N_KV_HEADS, Q_PER_KV, HEAD_DIM)
    k_s = k_s.reshape(B, L, N_KV_HEADS, HEAD_DIM)
    v_s = v_s.reshape(B, L, N_KV_HEADS, HEAD_DIM)
    q_a = rmsnorm(q_a.reshape(B, L, N_KV_HEADS, Q_PER_KV, HEAD_DIM), p["ax_q_norm"])
    k_a = rmsnorm(k_a.reshape(B, L, N_KV_HEADS, HEAD_DIM), p["ax_k_norm"])
    v_a = v_a.reshape(B, L, N_KV_HEADS, HEAD_DIM)
    sink = p["swa_sink"].reshape(N_KV_HEADS, Q_PER_KV)
    ssm_params = (p["ssm_lambda_re"], p["ssm_lambda_im"], p["ssm_b_re"], p["ssm_b_im"],
                  p["ssm_c_re"], p["ssm_c_im"], p["ssm_log_dt"], p["ssm_d"], p["ssm_w_glu"], p["ssm_b_glu"])
    if ctx is None:
        out_a, s_fin = ssm_mixer(u_ssm, *ssm_params, s0=None)
        out_b = sweep_attention(q_s, k_s, v_s, sink=sink)
        out_c = sweep_attention(q_a, k_a, v_a)
        new_ctx = (jnp.stack([k_s, v_s], axis=1),
                   jnp.stack([k_a, v_a], axis=1),
                   jnp.stack([jnp.real(s_fin), jnp.imag(s_fin)], axis=-1))
    else:
        swa_kv, ax_kv, ssm_state = ctx
        s0 = lax.complex(ssm_state[..., 0].astype(jnp.float32), ssm_state[..., 1].astype(jnp.float32))
        out_a, _ = ssm_mixer(u_ssm, *ssm_params, s0=s0)
        cos, sin = axial_rope_tables(L)
        out_b = sweep_attention(apply_rope(q_s, cos, sin), apply_rope(k_s, cos, sin), v_s, sink=sink, band=True,
                                q_ctx=q_s, k_ctx=swa_kv[:, 0], v_ctx=swa_kv[:, 1])
        out_c = sweep_attention(apply_rope(q_a, cos, sin), apply_rope(k_a, cos, sin), v_a,
                                q_ctx=q_a, k_ctx=ax_kv[:, 0], v_ctx=ax_kv[:, 1])
        new_ctx = None
    out_d = fourier_mixer(u_f, p["fnet_w"], p["fnet_b"])
    merged = jnp.concatenate([out_a, out_b.reshape(B, L, W_SWA), out_c.reshape(B, L, W_AX), out_d], axis=-1)
    return merged @ p["w_out"], new_ctx


def trunk_layer(h, cond, p, ctx):
    m = modulation(cond, p["w_mod"], p["b_mod"])
    sh1, sc1, g1, sh2, sc2, g2, sh3, sc3, g3 = [m[:, :, j] for j in range(N_MOD)]
    x = rmsnorm(h, p["norm_ffn1"]) * (1 + sc1) + sh1
    h = h + 0.5 * g1 * swiglu(x, p["ffn1_w_gate"], p["ffn1_w_up"], p["ffn1_w_down"])
    x = rmsnorm(h, p["norm_mix"]) * (1 + sc2) + sh2
    mixed, new_ctx = token_mixing(x, p, ctx)
    h = h + g2 * mixed
    x = rmsnorm(h, p["norm_ffn2"]) * (1 + sc3) + sh3
    h = h + 0.5 * g3 * swiglu(x, p["ffn2_w_gate"], p["ffn2_w_up"], p["ffn2_w_down"])
    return h, new_ctx


def setup_inputs(seed: int = 0) -> dict:
    key = jax.random.key(seed)
    ks = iter(jax.random.split(key, 48))

    def nrm(shape, scale=1.0):
        return jax.random.normal(next(ks), shape, jnp.float32) * scale

    def gain(shape):
        return 1.0 + nrm(shape, 0.05)

    lam_im = jnp.broadcast_to(math.pi * jnp.arange(SSM_STATE, dtype=jnp.float32),
                              (DEPTH, 2, N_SSM_GROUPS, SSM_STATE)) + nrm((DEPTH, 2, N_SSM_GROUPS, SSM_STATE), 0.01)
    log_dt = jax.random.uniform(next(ks), (DEPTH, 2, N_SSM_GROUPS), jnp.float32, math.log(1e-3), math.log(1e-1))
    return {
        "x_prompt": nrm((BATCH, SEQ, D_MODEL)),
        "x_sample": nrm((DEC_BATCH, DEC_SEQ, D_MODEL)),
        "cache_swa_kv": nrm((DEC_BATCH, DEPTH, 2, PAST_LEN, N_KV_HEADS, HEAD_DIM)),
        "cache_axial_kv": nrm((DEC_BATCH, DEPTH, 2, PAST_LEN, N_KV_HEADS, HEAD_DIM)),
        "state_ssm": nrm((DEC_BATCH, DEPTH, 2, N_SSM_GROUPS, SSM_STATE, 2), 0.1),
        "c": nrm((DEC_BATCH, D_MODEL)),
        "c_ctx": nrm((D_MODEL,)),
        "w_mod": nrm((DEPTH, D_MODEL, N_MOD * D_MODEL), 0.5 * D_MODEL ** -0.5),
        "b_mod": nrm((DEPTH, N_MOD * D_MODEL), 0.01),
        "norm_ffn1": gain((DEPTH, D_MODEL)),
        "norm_mix": gain((DEPTH, D_MODEL)),
        "norm_ffn2": gain((DEPTH, D_MODEL)),
        "ffn1_w_gate": nrm((DEPTH, D_MODEL, D_FF), D_MODEL ** -0.5),
        "ffn1_w_up": nrm((DEPTH, D_MODEL, D_FF), D_MODEL ** -0.5),
        "ffn1_w_down": nrm((DEPTH, D_FF, D_MODEL), D_FF ** -0.5),
        "ffn2_w_gate": nrm((DEPTH, D_MODEL, D_FF), D_MODEL ** -0.5),
        "ffn2_w_up": nrm((DEPTH, D_MODEL, D_FF), D_MODEL ** -0.5),
        "ffn2_w_down": nrm((DEPTH, D_FF, D_MODEL), D_FF ** -0.5),
        "w_in": nrm((DEPTH, D_MODEL, P_IN), D_MODEL ** -0.5),
        "w_out": nrm((DEPTH, D_MIX, D_MODEL), D_MIX ** -0.5),
        "ssm_lambda_re": -0.5 + nrm((DEPTH, 2, N_SSM_GROUPS, SSM_STATE), 0.02),
        "ssm_lambda_im": lam_im,
        "ssm_b_re": nrm((DEPTH, 2, N_SSM_GROUPS, SSM_STATE, SSM_GROUP), (2 * SSM_GROUP) ** -0.5),
        "ssm_b_im": nrm((DEPTH, 2, N_SSM_GROUPS, SSM_STATE, SSM_GROUP), (2 * SSM_GROUP) ** -0.5),
        "ssm_c_re": nrm((DEPTH, 2, N_SSM_GROUPS, SSM_GROUP, SSM_STATE), SSM_STATE ** -0.5),
        "ssm_c_im": nrm((DEPTH, 2, N_SSM_GROUPS, SSM_GROUP, SSM_STATE), SSM_STATE ** -0.5),
        "ssm_log_dt": log_dt,
        "ssm_d": nrm((DEPTH, W_SSM)),
        "ssm_w_glu": nrm((DEPTH, W_SSM, W_SSM), W_SSM ** -0.5),
        "ssm_b_glu": nrm((DEPTH, W_SSM), 0.01),
        "swa_sink": nrm((DEPTH, N_Q_HEADS)),
        "ax_q_norm": gain((DEPTH, HEAD_DIM)),
        "ax_k_norm": gain((DEPTH, HEAD_DIM)),
        "fnet_w": nrm((DEPTH, W_FNET, W_FNET), W_FNET ** -0.5),
        "fnet_b": nrm((DEPTH, W_FNET), 0.01),
        "final_norm": gain((D_MODEL,)),
    }


def reference(x_prompt, x_sample, cache_swa_kv, cache_axial_kv, state_ssm, c, c_ctx,
              w_mod, b_mod, norm_ffn1, norm_mix, norm_ffn2,
              ffn1_w_gate, ffn1_w_up, ffn1_w_down, ffn2_w_gate, ffn2_w_up, ffn2_w_down,
              w_in, w_out, ssm_lambda_re, ssm_lambda_im, ssm_b_re, ssm_b_im, ssm_c_re, ssm_c_im,
              ssm_log_dt, ssm_d, ssm_w_glu, ssm_b_glu, swa_sink, ax_q_norm, ax_k_norm,
              fnet_w, fnet_b, final_norm):
    h_ctx, h_lat = x_prompt, x_sample
    swa_list, ax_list, ssm_list = [], [], []
    for l in range(DEPTH):
        p = dict(w_mod=w_mod[l], b_mod=b_mod[l], norm_ffn1=norm_ffn1[l], norm_mix=norm_mix[l],
                 norm_ffn2=norm_ffn2[l], ffn1_w_gate=ffn1_w_gate[l], ffn1_w_up=ffn1_w_up[l],
                 ffn1_w_down=ffn1_w_down[l], ffn2_w_gate=ffn2_w_gate[l], ffn2_w_up=ffn2_w_up[l],
                 ffn2_w_down=ffn2_w_down[l], w_in=w_in[l], w_out=w_out[l],
                 ssm_lambda_re=ssm_lambda_re[l], ssm_lambda_im=ssm_lambda_im[l],
                 ssm_b_re=ssm_b_re[l], ssm_b_im=ssm_b_im[l], ssm_c_re=ssm_c_re[l], ssm_c_im=ssm_c_im[l],
                 ssm_log_dt=ssm_log_dt[l], ssm_d=ssm_d[l], ssm_w_glu=ssm_w_glu[l], ssm_b_glu=ssm_b_glu[l],
                 swa_sink=swa_sink[l], ax_q_norm=ax_q_norm[l], ax_k_norm=ax_k_norm[l],
                 fnet_w=fnet_w[l], fnet_b=fnet_b[l])
        h_ctx, (swa_kv, ax_kv, s_state) = trunk_layer(h_ctx, c_ctx[None, :], p, None)
        swa_list.append(swa_kv)
        ax_list.append(ax_kv)
        ssm_list.append(s_state)
        h_lat, _ = trunk_layer(h_lat, c, p, (cache_swa_kv[:, l], cache_axial_kv[:, l], state_ssm[:, l]))
    y_prompt = rmsnorm(h_ctx, final_norm)
    y_sample = rmsnorm(h_lat, final_norm)
    new_swa_kv = jnp.stack(swa_list, axis=1)
    new_axial_kv = jnp.stack(ax_list, axis=1)
    new_state_ssm = jnp.stack(ssm_list, axis=1)
    return (y_prompt, y_sample, new_swa_kv, new_axial_kv, new_state_ssm)
```

```python
import functools
import math

import jax
import jax.numpy as jnp
from jax import lax
from jax.experimental import pallas as pl
from jax.experimental.pallas import tpu as pltpu

F32 = jnp.float32
BF16 = jnp.bfloat16

D_MODEL = 1024
D_FF = 2816
N_MOD = 9
EPS = 1e-6
HEAD_DIM = 64
HALF_HEAD = HEAD_DIM // 2
N_Q_HEADS = 4
Q_PER_KV = 2
W_GRP = 256
KV_W = 128
P_IN = 1536
N_SSM_GROUPS = 16
SSM_GROUP = 16
SSM_STATE = 64
N_STATE = N_SSM_GROUPS * SSM_STATE
WINDOW = 128
GRID_W = 64
ROPE_BASE = 10000.0
NEG_BIG = -1e30

FF_CHUNK = 1408
SUBLANES = 8
DFT_RADIX = 64
VMEM_LIMIT = 56 * 1024 * 1024


def _cparams(sem):
    return pltpu.CompilerParams(dimension_semantics=sem, vmem_limit_bytes=VMEM_LIMIT)


def _adaln(h, g, sc, sh):
    ms = jnp.mean(h * h, axis=-1, keepdims=True)
    return (h * lax.rsqrt(ms + EPS) * g) * (1.0 + sc) + sh


def _mod_kernel(c_ref, w_ref, b_ref, o_ref):
    c = c_ref[...]
    a = (c * jax.nn.sigmoid(c)).astype(BF16)
    o_ref[...] = jnp.dot(a, w_ref[...].astype(BF16), preferred_element_type=F32) + b_ref[...]


def _modulation(cond, w_mod, b_mod):
    depth = w_mod.shape[0]
    n = N_MOD * D_MODEL
    tn = 1024
    return pl.pallas_call(
        _mod_kernel,
        grid=(depth, n // tn),
        in_specs=[
            pl.BlockSpec((16, D_MODEL), lambda l, j: (0, 0)),
            pl.BlockSpec((None, D_MODEL, tn), lambda l, j: (l, 0, j)),
            pl.BlockSpec((None, 1, tn), lambda l, j: (l, 0, j)),
        ],
        out_specs=pl.BlockSpec((None, 16, tn), lambda l, j: (l, 0, j)),
        out_shape=jax.ShapeDtypeStruct((depth, 16, n), F32),
        compiler_params=_cparams(("parallel", "parallel")),
        name="modulation",
    )(cond, w_mod, b_mod)


def _ffn_kernel(h_ref, mod_ref, g_ref, wg_ref, wu_ref, wd_ref, fg_ref, o_ref, x_scr, acc_scr,
                *, mod_base, final_norm):
    k = pl.program_id(1)

    @pl.when(k == 0)
    def _():
        x = _adaln(h_ref[...], g_ref[...], mod_ref[0, mod_base + 1:mod_base + 2, :],
                   mod_ref[0, mod_base:mod_base + 1, :])
        x_scr[...] = x.astype(BF16)

    x = x_scr[...]
    gate = jnp.dot(x, wg_ref[...], preferred_element_type=F32)
    up = jnp.dot(x, wu_ref[...], preferred_element_type=F32)
    a = (gate * jax.nn.sigmoid(gate) * up).astype(BF16)
    part = jnp.dot(a, wd_ref[...], preferred_element_type=F32)

    @pl.when(k == 0)
    def _():
        acc_scr[...] = part

    @pl.when(k > 0)
    def _():
        acc_scr[...] += part

    @pl.when(k == pl.num_programs(1) - 1)
    def _():
        gmod = mod_ref[0, mod_base + 2:mod_base + 3, :]
        hn = h_ref[...] + (0.5 * gmod) * acc_scr[...]
        if final_norm:
            ms = jnp.mean(hn * hn, axis=-1, keepdims=True)
            hn = hn * lax.rsqrt(ms + EPS) * fg_ref[...]
        o_ref[...] = hn


def _ffn(h, mod, g, wg, wu, wd, fg, *, mod_base, tm, tiles_per_cond, cond_base, final_norm):
    t = h.shape[0]
    cond = lambda i, k: (cond_base + i // tiles_per_cond, 0, 0)
    kern = functools.partial(_ffn_kernel, mod_base=mod_base, final_norm=final_norm)
    return pl.pallas_call(
        kern,
        grid=(t // tm, D_FF // FF_CHUNK),
        in_specs=[
            pl.BlockSpec((tm, D_MODEL), lambda i, k: (i, 0)),
            pl.BlockSpec((1, N_MOD, D_MODEL), cond),
            pl.BlockSpec((1, D_MODEL), lambda i, k: (0, 0)),
            pl.BlockSpec((D_MODEL, FF_CHUNK), lambda i, k: (0, k)),
            pl.BlockSpec((D_MODEL, FF_CHUNK), lambda i, k: (0, k)),
            pl.BlockSpec((FF_CHUNK, D_MODEL), lambda i, k: (k, 0)),
            pl.BlockSpec((1, D_MODEL), lambda i, k: (0, 0)),
        ],
        out_specs=pl.BlockSpec((tm, D_MODEL), lambda i, k: (i, 0)),
        out_shape=jax.ShapeDtypeStruct((t, D_MODEL), F32),
        scratch_shapes=[pltpu.VMEM((tm, D_MODEL), BF16), pltpu.VMEM((tm, D_MODEL), F32)],
        compiler_params=_cparams(("parallel", "arbitrary")),
        name="ffn",
    )(h, mod, g, wg, wu, wd, fg)


def _head_mean_sq(x, e):
    x2 = x * x
    hi = x2.astype(BF16)
    lo = (x2 - hi.astype(F32)).astype(BF16)
    return jnp.dot(hi, e, preferred_element_type=F32) + jnp.dot(lo, e, preferred_element_type=F32)


def _swap_halves(x):
    w = x.shape[-1]
    lane = lax.broadcasted_iota(jnp.int32, x.shape, 1)
    first = (lane & HALF_HEAD) == 0
    return jnp.where(first, pltpu.roll(x, w - HALF_HEAD, 1), pltpu.roll(x, HALF_HEAD, 1))


def _mix_in_kernel(*refs, rope):
    if rope:
        (h_ref, mod_ref, g_ref, w_ref, qn_ref, kn_ref, e_ref, cs_ref, cos_ref, sin_ref,
         u_ref, qkv_ref, pc_ref, ps_ref) = refs
    else:
        (h_ref, mod_ref, g_ref, w_ref, qn_ref, kn_ref, e_ref, cs_ref,
         u_ref, qkv_ref, pc_ref, ps_ref, kv_ref) = refs
    x = _adaln(h_ref[...], g_ref[...], mod_ref[0, 4:5, :], mod_ref[0, 3:4, :]).astype(BF16)
    proj = jnp.dot(x, w_ref[...], preferred_element_type=F32)
    u_ssm = proj[:, 0:256]
    q_s = proj[:, 256:512]
    k_s = proj[:, 512:640]
    v_s = proj[:, 640:768]
    q_a = proj[:, 768:1024]
    k_a = proj[:, 1024:1152]
    v_a = proj[:, 1152:1280]
    u_f = proj[:, 1280:1536]

    e = e_ref[...]
    q_a = q_a * lax.rsqrt(_head_mean_sq(q_a, e) + EPS) * qn_ref[...]
    k_a = k_a * lax.rsqrt(_head_mean_sq(k_a, e[0:KV_W, 0:KV_W]) + EPS) * kn_ref[...]

    u_ref[...] = u_ssm
    p = jnp.dot(u_f.astype(BF16), cs_ref[...], preferred_element_type=F32)
    pc_ref[...] = p[:, 0:256].astype(BF16)
    ps_ref[...] = p[:, 256:512].astype(BF16)

    scale = HEAD_DIM ** -0.5
    if rope:
        cos = cos_ref[...]
        sin = sin_ref[...]
        rot = lambda t, w: t * cos[:, 0:w] + _swap_halves(t) * sin[:, 0:w]
        qkv_ref[:, 0:256] = (q_s * scale).astype(BF16)
        qkv_ref[:, 256:512] = (rot(q_s, 256) * scale).astype(BF16)
        qkv_ref[:, 512:768] = (q_a * scale).astype(BF16)
        qkv_ref[:, 768:1024] = (rot(q_a, 256) * scale).astype(BF16)
        qkv_ref[:, 1024:1152] = rot(k_s, KV_W).astype(BF16)
        qkv_ref[:, 1152:1280] = v_s.astype(BF16)
        qkv_ref[:, 1280:1408] = rot(k_a, KV_W).astype(BF16)
        qkv_ref[:, 1408:1536] = v_a.astype(BF16)
    else:
        qkv_ref[:, 0:256] = (q_s * scale).astype(BF16)
        qkv_ref[:, 256:384] = k_s.astype(BF16)
        qkv_ref[:, 384:512] = v_s.astype(BF16)
        qkv_ref[:, 512:768] = (q_a * scale).astype(BF16)
        qkv_ref[:, 768:896] = k_a.astype(BF16)
        qkv_ref[:, 896:1024] = v_a.astype(BF16)
        kv_ref[0] = k_s
        kv_ref[1] = v_s
        kv_ref[2] = k_a
        kv_ref[3] = v_a


def _mix_in(h, mod, g, w_in, qn, kn, e, cs, rope_tabs, *, batch, seq, tm, cond_base, rope):
    t = batch * seq
    tiles = seq // tm
    nc = batch * W_GRP
    cond = lambda i: (cond_base + (i // tiles if rope else 0), 0, 0)
    tcol = lambda i: (i % tiles, i // tiles)
    const = lambda i: (0, 0)
    in_specs = [
        pl.BlockSpec((tm, D_MODEL), lambda i: (i, 0)),
        pl.BlockSpec((1, N_MOD, D_MODEL), cond),
        pl.BlockSpec((1, D_MODEL), const),
        pl.BlockSpec((D_MODEL, P_IN), const),
        pl.BlockSpec((1, 256), const),
        pl.BlockSpec((1, KV_W), const),
        pl.BlockSpec((256, 256), const),
        pl.BlockSpec((256, 512), const),
    ]
    args = [h, mod, g, w_in, qn, kn, e, cs]
    qkv_w = 1536 if rope else 1024
    out_specs = [
        pl.BlockSpec((tm, W_GRP), tcol),
        pl.BlockSpec((tm, qkv_w), lambda i: (i, 0)),
        pl.BlockSpec((tm, W_GRP), tcol),
        pl.BlockSpec((tm, W_GRP), tcol),
    ]
    out_shape = [
        jax.ShapeDtypeStruct((seq, nc), F32),
        jax.ShapeDtypeStruct((t, qkv_w), BF16),
        jax.ShapeDtypeStruct((seq, nc), BF16),
        jax.ShapeDtypeStruct((seq, nc), BF16),
    ]
    if rope:
        in_specs += [pl.BlockSpec((tm, 256), lambda i: (i % tiles, 0))] * 2
        args += list(rope_tabs)
    else:
        assert tm == seq
        out_specs.append(pl.BlockSpec((None, 4, seq, KV_W), lambda i: (i, 0, 0, 0)))
        out_shape.append(jax.ShapeDtypeStruct((batch, 4, seq, KV_W), F32))
    return pl.pallas_call(
        functools.partial(_mix_in_kernel, rope=rope),
        grid=(t // tm,),
        in_specs=in_specs,
        out_specs=out_specs,
        out_shape=out_shape,
        compiler_params=_cparams(("parallel",)),
        name="mix_in",
    )(*args)


def _softmax_pv(parts, sink):
    m = None
    for s, _ in parts:
        mi = jnp.max(s, axis=-1, keepdims=True)
        m = mi if m is None else jnp.maximum(m, mi)
    if sink is not None:
        m = jnp.maximum(m, sink)
    l = None
    o = None
    for s, v in parts:
        p = jnp.exp(s - m)
        li = jnp.sum(p, axis=-1, keepdims=True)
        oi = jnp.dot(p.astype(BF16), v, preferred_element_type=F32)
        l = li if l is None else l + li
        o = oi if o is None else o + oi
    if sink is not None:
        l = l + jnp.exp(sink - m)
    return o / l


def _qk(q, k):
    return lax.dot_general(q, k, (((1,), (1,)), ((), ())), preferred_element_type=F32)


def _attn_ctx_kernel(sink_ref, qkv_ref, o_ref):
    for grp in range(2):
        base = grp * 512
        for h in range(N_Q_HEADS):
            kv = h // Q_PER_KV
            q = qkv_ref[:, base + h * HEAD_DIM:base + (h + 1) * HEAD_DIM]
            k = qkv_ref[:, base + 256 + kv * HEAD_DIM:base + 256 + (kv + 1) * HEAD_DIM]
            v = qkv_ref[:, base + 384 + kv * HEAD_DIM:base + 384 + (kv + 1) * HEAD_DIM]
            sink = sink_ref[h] if grp == 0 else None
            o = _softmax_pv([(_qk(q, k), v)], sink)
            o_ref[:, grp * 256 + h * HEAD_DIM:grp * 256 + (h + 1) * HEAD_DIM] = o.astype(BF16)


def _attn_ctx(sink, qkv, *, batch, seq):
    return pl.pallas_call(
        _attn_ctx_kernel,
        grid=(batch,),
        in_specs=[
            pl.BlockSpec(memory_space=pltpu.SMEM),
            pl.BlockSpec((seq, 1024), lambda b: (b, 0)),
        ],
        out_specs=pl.BlockSpec((seq, 512), lambda b: (b, 0)),
        out_shape=jax.ShapeDtypeStruct((batch * seq, 512), BF16),
        compiler_params=_cparams(("parallel",)),
        name="attn_ctx",
    )(sink, qkv)


def _attn_lat_kernel(sink_ref, q_ref, kv_ref, cs_ref, ca_ref, o_ref, *, tq, seq):
    i = pl.program_id(1)
    q0 = i * tq
    win = tq + 2 * WINDOW
    ws = pl.multiple_of(jnp.clip(q0 - WINDOW, 0, seq - win), WINDOW)
    qpos = q0 + lax.broadcasted_iota(jnp.int32, (tq, win), 0)
    kpos = ws + lax.broadcasted_iota(jnp.int32, (tq, win), 1)
    in_band = jnp.abs(qpos - kpos) <= WINDOW
    for h in range(N_Q_HEADS):
        kv = h // Q_PER_KV
        hs = slice(h * HEAD_DIM, (h + 1) * HEAD_DIM)
        kw = kv_ref[pl.ds(ws, win), kv * HEAD_DIM:(kv + 1) * HEAD_DIM]
        vw = kv_ref[pl.ds(ws, win), KV_W + kv * HEAD_DIM:KV_W + (kv + 1) * HEAD_DIM]
        s_main = jnp.where(in_band, _qk(q_ref[:, 256 + h * HEAD_DIM:256 + (h + 1) * HEAD_DIM], kw), NEG_BIG)
        kc = cs_ref[0, :, kv * HEAD_DIM:(kv + 1) * HEAD_DIM]
        vc = cs_ref[1, :, kv * HEAD_DIM:(kv + 1) * HEAD_DIM]
        s_ctx = _qk(q_ref[:, hs], kc)
        o = _softmax_pv([(s_main, vw), (s_ctx, vc)], sink_ref[h])
        o_ref[:, hs] = o.astype(BF16)
        ka = kv_ref[:, 256 + kv * HEAD_DIM:256 + (kv + 1) * HEAD_DIM]
        va = kv_ref[:, 384 + kv * HEAD_DIM:384 + (kv + 1) * HEAD_DIM]
        s_main = _qk(q_ref[:, 768 + h * HEAD_DIM:768 + (h + 1) * HEAD_DIM], ka)
        kc = ca_ref[0, :, kv * HEAD_DIM:(kv + 1) * HEAD_DIM]
        vc = ca_ref[1, :, kv * HEAD_DIM:(kv + 1) * HEAD_DIM]
        s_ctx = _qk(q_ref[:, 512 + h * HEAD_DIM:512 + (h + 1) * HEAD_DIM], kc)
        o = _softmax_pv([(s_main, va), (s_ctx, vc)], None)
        o_ref[:, 256 + h * HEAD_DIM:256 + (h + 1) * HEAD_DIM] = o.astype(BF16)


def _attn_lat(sink, qkv, cache_s, cache_a, *, batch, seq, tq):
    nq = seq // tq
    past = cache_s.shape[2]
    return pl.pallas_call(
        functools.partial(_attn_lat_kernel, tq=tq, seq=seq),
        grid=(batch, nq),
        in_specs=[
            pl.BlockSpec(memory_space=pltpu.SMEM),
            pl.BlockSpec((tq, 1024), lambda b, i: (b * nq + i, 0)),
            pl.BlockSpec((seq, 512), lambda b, i: (b, 2)),
            pl.BlockSpec((None, 2, past, KV_W), lambda b, i: (b, 0, 0, 0)),
            pl.BlockSpec((None, 2, past, KV_W), lambda b, i: (b, 0, 0, 0)),
        ],
        out_specs=pl.BlockSpec((tq, 512), lambda b, i: (b * nq + i, 0)),
        out_shape=jax.ShapeDtypeStruct((batch * seq, 512), BF16),
        compiler_params=_cparams(("parallel", "parallel")),
        name="attn_lat",
    )(sink, qkv, qkv, cache_s, cache_a)


def _ssm_kernel(u_ref, bm_ref, cm_ref, a_ref, s0_ref, y_ref, fin_ref, bu_scr, st_scr, *, lc):
    d = pl.program_id(0)
    n = pl.program_id(2)

    @pl.when(n == 0)
    def _():
        st_scr[...] = s0_ref[...]

    u = u_ref[...].reshape(lc * SUBLANES, W_GRP).astype(BF16)
    bu_scr[...] = jnp.dot(u, bm_ref[...], preferred_element_type=F32)
    ar = jnp.broadcast_to(a_ref[0:1, :], (SUBLANES, N_STATE))
    ai = jnp.broadcast_to(a_ref[1:2, :], (SUBLANES, N_STATE))

    def step(t, carry):
        sr, si = carry
        tt = jnp.where(d == 0, t, lc - 1 - t)
        r0 = pl.multiple_of(tt * SUBLANES, SUBLANES)
        br = bu_scr[pl.ds(r0, SUBLANES), 0:N_STATE]
        bi = bu_scr[pl.ds(r0, SUBLANES), N_STATE:2 * N_STATE]
        nr = ar * sr - ai * si + br
        ni = ar * si + ai * sr + bi
        bu_scr[pl.ds(r0, SUBLANES), 0:N_STATE] = nr
        bu_scr[pl.ds(r0, SUBLANES), N_STATE:2 * N_STATE] = ni
        return nr, ni

    sr, si = lax.fori_loop(0, lc, step, (st_scr[:, 0:N_STATE], st_scr[:, N_STATE:2 * N_STATE]), unroll=8)
    st_scr[:, 0:N_STATE] = sr
    st_scr[:, N_STATE:2 * N_STATE] = si
    y = jnp.dot(bu_scr[...].astype(BF16), cm_ref[...], preferred_element_type=F32)
    y_ref[...] = y.reshape(lc, SUBLANES, W_GRP)

    @pl.when(n == pl.num_programs(2) - 1)
    def _():
        fin_ref[...] = st_scr[...]


def _ssm(u_t, bmat, cmat, a, s0, *, batch, seq, lc):
    nch = seq // lc
    chunk = lambda d, n: n + d * (nch - 1 - 2 * n)
    return pl.pallas_call(
        functools.partial(_ssm_kernel, lc=lc),
        grid=(2, batch // SUBLANES, nch),
        in_specs=[
            pl.BlockSpec((lc, SUBLANES, W_GRP), lambda d, b, n: (chunk(d, n), b, 0)),
            pl.BlockSpec((None, W_GRP, 2 * N_STATE), lambda d, b, n: (d, 0, 0)),
            pl.BlockSpec((None, 2 * N_STATE, W_GRP), lambda d, b, n: (d, 0, 0)),
            pl.BlockSpec((None, 2, N_STATE), lambda d, b, n: (d, 0, 0)),
            pl.BlockSpec((None, SUBLANES, 2 * N_STATE), lambda d, b, n: (d, b, 0)),
        ],
        out_specs=[
            pl.BlockSpec((None, lc, SUBLANES, W_GRP), lambda d, b, n: (d, chunk(d, n), b, 0)),
            pl.BlockSpec((None, SUBLANES, 2 * N_STATE), lambda d, b, n: (d, b, 0)),
        ],
        out_shape=[
            jax.ShapeDtypeStruct((2, seq, batch, W_GRP), F32),
            jax.ShapeDtypeStruct((2, batch, 2 * N_STATE), F32),
        ],
        scratch_shapes=[
            pltpu.VMEM((lc * SUBLANES, 2 * N_STATE), F32),
            pltpu.VMEM((SUBLANES, 2 * N_STATE), F32),
        ],
        compiler_params=_cparams(("parallel", "parallel", "arbitrary")),
        name="ssm_scan",
    )(u_t, bmat, cmat, a, s0)


def _dft_kernel(t1c_ref, t1s_ref, t2c_ref, t2s_ref, pc_ref, ps_ref, o_ref, lc_scr, ls_scr, acc_scr,
                *, tm, scale):
    kb = pl.program_id(2)
    t2c = t2c_ref[...]
    t2s = t2s_ref[...]
    for a in range(tm // DFT_RADIX):
        c1 = t1c_ref[a:a + 1, :]
        s1 = t1s_ref[a:a + 1, :]
        rows = slice(a * DFT_RADIX, (a + 1) * DFT_RADIX)
        lc_scr[rows, :] = (c1 * t2c - s1 * t2s).astype(BF16)
        ls_scr[rows, :] = (s1 * t2c + c1 * t2s).astype(BF16)
    part = (jnp.dot(lc_scr[...], pc_ref[...], preferred_element_type=F32)
            - jnp.dot(ls_scr[...], ps_ref[...], preferred_element_type=F32))

    @pl.when(kb == 0)
    def _():
        acc_scr[...] = part

    @pl.when(kb > 0)
    def _():
        acc_scr[...] += part

    @pl.when(kb == pl.num_programs(2) - 1)
    def _():
        o_ref[...] = (acc_scr[...] * scale).astype(BF16)


def _dft(tabs, pc, ps, *, seq, tm, tk, tn):
    nc = pc.shape[1]
    t1c, t1s, t2c, t2s = tabs
    ra = tm // DFT_RADIX
    scale = 1.0 / math.sqrt(seq * W_GRP)
    return pl.pallas_call(
        functools.partial(_dft_kernel, tm=tm, scale=scale),
        grid=(seq // tm, nc // tn, seq // tk),
        in_specs=[
            pl.BlockSpec((ra, tk), lambda i, j, k: (i, k)),
            pl.BlockSpec((ra, tk), lambda i, j, k: (i, k)),
            pl.BlockSpec((DFT_RADIX, tk), lambda i, j, k: (0, k)),
            pl.BlockSpec((DFT_RADIX, tk), lambda i, j, k: (0, k)),
            pl.BlockSpec((tk, tn), lambda i, j, k: (k, j)),
            pl.BlockSpec((tk, tn), lambda i, j, k: (k, j)),
        ],
        out_specs=pl.BlockSpec((tm, tn), lambda i, j, k: (i, j)),
        out_shape=jax.ShapeDtypeStruct((seq, nc), BF16),
        scratch_shapes=[pltpu.VMEM((tm, tk), BF16), pltpu.VMEM((tm, tk), BF16), pltpu.VMEM((tm, tn), F32)],
        compiler_params=_cparams(("parallel", "parallel", "arbitrary")),
        name="pos_dft",
    )(t1c, t1s, t2c, t2s, pc, ps)


def _mix_out_kernel(h_ref, mod_ref, yf_ref, yb_ref, u_ref, att_ref, f_ref, d_ref, wglu_ref, bglu_ref,
                    wf_ref, bf_ref, wo_ref, o_ref, m_scr):
    y = d_ref[...] * u_ref[...] + yf_ref[...] + yb_ref[...]
    y = y * (0.5 * (1.0 + jnp.tanh(math.sqrt(2.0 / math.pi) * (y + 0.044715 * (y * y * y)))))
    z = jnp.dot(y.astype(BF16), wglu_ref[...], preferred_element_type=F32) + bglu_ref[...]
    m_scr[:, 0:256] = (y * jax.nn.sigmoid(z)).astype(BF16)
    m_scr[:, 256:768] = att_ref[...]
    m_scr[:, 768:1024] = (jnp.dot(f_ref[...], wf_ref[...], preferred_element_type=F32) + bf_ref[...]).astype(BF16)
    mixed = jnp.dot(m_scr[...], wo_ref[...], preferred_element_type=F32)
    o_ref[...] = h_ref[...] + mod_ref[0, 5:6, :] * mixed


def _mix_out(h, mod, y_dir, u_t, att, f, d, wglu, bglu, wf, bfn, wo, *, batch, seq, tm, cond_base, per_batch_cond):
    t = batch * seq
    tiles = seq // tm
    cond = lambda i: (cond_base + (i // tiles if per_batch_cond else 0), 0, 0)
    tcol = lambda i: (i % tiles, i // tiles)
    const = lambda i: (0, 0)
    return pl.pallas_call(
        _mix_out_kernel,
        grid=(t // tm,),
        in_specs=[
            pl.BlockSpec((tm, D_MODEL), lambda i: (i, 0)),
            pl.BlockSpec((1, N_MOD, D_MODEL), cond),
            pl.BlockSpec((None, tm, W_GRP), lambda i: (0, i % tiles, i // tiles)),
            pl.BlockSpec((None, tm, W_GRP), lambda i: (1, i % tiles, i // tiles)),
            pl.BlockSpec((tm, W_GRP), tcol),
            pl.BlockSpec((tm, 512), lambda i: (i, 0)),
            pl.BlockSpec((tm, W_GRP), tcol),
            pl.BlockSpec((1, W_GRP), const),
            pl.BlockSpec((W_GRP, W_GRP), const),
            pl.BlockSpec((1, W_GRP), const),
            pl.BlockSpec((W_GRP, W_GRP), const),
            pl.BlockSpec((1, W_GRP), const),
            pl.BlockSpec((D_MODEL, D_MODEL), const),
        ],
        out_specs=pl.BlockSpec((tm, D_MODEL), lambda i: (i, 0)),
        out_shape=jax.ShapeDtypeStruct((t, D_MODEL), F32),
        scratch_shapes=[pltpu.VMEM((tm, D_MODEL), BF16)],
        compiler_params=_cparams(("parallel",)),
        name="mix_out",
    )(h, mod, y_dir, y_dir, u_t, att, f, d, wglu, bglu, wf, bfn, wo)


def _rope_tables(seq):
    rows = seq // GRID_W
    row_id = jnp.repeat(jnp.arange(rows), GRID_W).astype(F32)
    col_id = jnp.tile(jnp.arange(GRID_W), rows).astype(F32)
    n_freq = HEAD_DIM // 4
    inv = ROPE_BASE ** (-jnp.arange(n_freq, dtype=F32) / n_freq)
    ang = jnp.concatenate([row_id[:, None] * inv, col_id[:, None] * inv], axis=-1)
    cos, sin = jnp.cos(ang), jnp.sin(ang)
    cos_full = jnp.tile(jnp.concatenate([cos, cos], axis=-1), (1, N_Q_HEADS))
    sin_signed = jnp.tile(jnp.concatenate([-sin, sin], axis=-1), (1, N_Q_HEADS))
    return cos_full, sin_signed


def _angle_table(mult, n, period):
    m = (mult[:, None] * n[None, :]) % period
    th = m.astype(F32) * (2.0 * math.pi / period)
    return jnp.cos(th), jnp.sin(th)


def _dft_tables(seq):
    n = jnp.arange(seq, dtype=jnp.int32)
    t1c, t1s = _angle_table(DFT_RADIX * jnp.arange(seq // DFT_RADIX, dtype=jnp.int32), n, seq)
    t2c, t2s = _angle_table(jnp.arange(DFT_RADIX, dtype=jnp.int32), n, seq)
    return t1c, t1s, t2c, t2s


def _ssm_params(lam_re, lam_im, b_re, b_im, c_re, c_im, log_dt):
    lam = lax.complex(lam_re, lam_im)
    dt = jnp.exp(log_dt)[..., None]
    a = jnp.exp(lam * dt)
    b_bar = ((a - 1.0) / lam)[..., None] * lax.complex(b_re, b_im)
    eye = jnp.eye(N_SSM_GROUPS, dtype=F32)

    def in_layout(x):
        y = jnp.transpose(x, (0, 1, 3, 2))[:, :, :, None, :] * eye[None, :, None, :, None]
        return y.reshape(2, N_SSM_GROUPS * SSM_GROUP, N_STATE)

    def out_layout(x):
        y = jnp.transpose(x, (0, 1, 3, 2))[:, :, :, None, :] * eye[None, :, None, :, None]
        return y.reshape(2, N_STATE, N_SSM_GROUPS * SSM_GROUP)

    bmat = jnp.concatenate([in_layout(jnp.real(b_bar)), in_layout(jnp.imag(b_bar))], axis=-1).astype(BF16)
    cmat = jnp.concatenate([out_layout(c_re), out_layout(-c_im)], axis=1).astype(BF16)
    a2 = jnp.stack([jnp.real(a).reshape(2, N_STATE), jnp.imag(a).reshape(2, N_STATE)], axis=1)
    return a2, bmat, cmat


def kernel(x_prompt, x_sample, cache_swa_kv, cache_axial_kv, state_ssm, c, c_ctx, w_mod, b_mod, norm_ffn1, norm_mix, norm_ffn2, ffn1_w_gate, ffn1_w_up, ffn1_w_down, ffn2_w_gate, ffn2_w_up, ffn2_w_down, w_in, w_out, ssm_lambda_re, ssm_lambda_im, ssm_b_re, ssm_b_im, ssm_c_re, ssm_c_im, ssm_log_dt, ssm_d, ssm_w_glu, ssm_b_glu, swa_sink, ax_q_norm, ax_k_norm, fnet_w, fnet_b, final_norm):
    depth = w_mod.shape[0]
    cb, cl, _ = x_prompt.shape
    lb, ll, _ = x_sample.shape
    past = cache_swa_kv.shape[3]

    cond = jnp.zeros((16, D_MODEL), F32).at[0].set(c_ctx).at[1:1 + lb].set(c)
    mod_all = _modulation(cond, w_mod, b_mod.reshape(depth, 1, N_MOD * D_MODEL))
    mod_all = mod_all.reshape(depth, 16, N_MOD, D_MODEL)

    e_heads = jnp.kron(jnp.eye(256 // HEAD_DIM, dtype=F32), jnp.ones((HEAD_DIM, HEAD_DIM), F32)) / HEAD_DIM
    e_heads = e_heads.astype(BF16)
    kc = jnp.arange(W_GRP, dtype=jnp.int32)
    cc, sc = _angle_table(kc, kc, W_GRP)
    cs_chan = jnp.concatenate([cc, sc], axis=-1).astype(BF16)
    rope_tabs = _rope_tables(ll)
    dft_ctx = _dft_tables(cl)
    dft_lat = _dft_tables(ll)
    fg = final_norm.reshape(1, D_MODEL)

    h_ctx = x_prompt.reshape(cb * cl, D_MODEL)
    h_lat = x_sample.reshape(lb * ll, D_MODEL)
    tm_lat = 512
    swa_list, ax_list, ssm_list = [], [], []
    for l in range(depth):
        mod = mod_all[l]
        bf = lambda w: w[l].astype(BF16)
        row = lambda v: v[l].reshape(1, -1)
        ffn1 = (row(norm_ffn1), bf(ffn1_w_gate), bf(ffn1_w_up), bf(ffn1_w_down), fg)
        ffn2 = (row(norm_ffn2), bf(ffn2_w_gate), bf(ffn2_w_up), bf(ffn2_w_down), fg)
        w_in_l, w_out_l = bf(w_in), bf(w_out)
        qn = jnp.tile(ax_q_norm[l], N_Q_HEADS).reshape(1, 256)
        kn = jnp.tile(ax_k_norm[l], KV_W // HEAD_DIM).reshape(1, KV_W)
        a2, bmat, cmat = _ssm_params(ssm_lambda_re[l], ssm_lambda_im[l], ssm_b_re[l], ssm_b_im[l],
                                     ssm_c_re[l], ssm_c_im[l], ssm_log_dt[l])
        mix_out_w = (row(ssm_d), bf(ssm_w_glu), row(ssm_b_glu), bf(fnet_w), row(fnet_b), w_out_l)
        sink = swa_sink[l]
        last = l == depth - 1

        h_ctx = _ffn(h_ctx, mod, *ffn1, mod_base=0, tm=512, tiles_per_cond=1 << 30, cond_base=0, final_norm=False)
        u_t, qkv, pc, ps, kv_new = _mix_in(h_ctx, mod, row(norm_mix), w_in_l, qn, kn, e_heads, cs_chan, None,
                                           batch=cb, seq=cl, tm=cl, cond_base=0, rope=False)
        y_dir, s_fin = _ssm(u_t.reshape(cl, cb, W_GRP), bmat, cmat, a2,
                            jnp.zeros((2, cb, 2 * N_STATE), F32), batch=cb, seq=cl, lc=128)
        att = _attn_ctx(sink, qkv, batch=cb, seq=cl)
        f = _dft(dft_ctx, pc, ps, seq=cl, tm=cl, tk=cl, tn=2048)
        h_ctx = _mix_out(h_ctx, mod, y_dir.reshape(2, cl, cb * W_GRP), u_t, att, f, *mix_out_w,
                         batch=cb, seq=cl, tm=cl, cond_base=0, per_batch_cond=False)
        h_ctx = _ffn(h_ctx, mod, *ffn2, mod_base=6, tm=512, tiles_per_cond=1 << 30, cond_base=0, final_norm=last)
        swa_list.append(kv_new[:, 0:2].reshape(cb, 2, cl, KV_W // HEAD_DIM, HEAD_DIM))
        ax_list.append(kv_new[:, 2:4].reshape(cb, 2, cl, KV_W // HEAD_DIM, HEAD_DIM))
        s_fin = s_fin.reshape(2, cb, 2, N_SSM_GROUPS, SSM_STATE)
        ssm_list.append(jnp.transpose(s_fin, (1, 0, 3, 4, 2)))

        tpc = ll // tm_lat
        h_lat = _ffn(h_lat, mod, *ffn1, mod_base=0, tm=tm_lat, tiles_per_cond=tpc, cond_base=1, final_norm=False)
        u_t, qkv, pc, ps = _mix_in(h_lat, mod, row(norm_mix), w_in_l, qn, kn, e_heads, cs_chan, rope_tabs,
                                   batch=lb, seq=ll, tm=tm_lat, cond_base=1, rope=True)
        s0 = jnp.transpose(state_ssm[:, l], (1, 0, 4, 2, 3)).reshape(2, lb, 2 * N_STATE)
        y_dir, _ = _ssm(u_t.reshape(ll, lb, W_GRP), bmat, cmat, a2, s0, batch=lb, seq=ll, lc=128)
        cache_s = cache_swa_kv[:, l].reshape(lb, 2, past, KV_W).astype(BF16)
        cache_a = cache_axial_kv[:, l].reshape(lb, 2, past, KV_W).astype(BF16)
        att = _attn_lat(sink, qkv, cache_s, cache_a, batch=lb, seq=ll, tq=256)
        f = _dft(dft_lat, pc, ps, seq=ll, tm=512, tk=512, tn=2048)
        h_lat = _mix_out(h_lat, mod, y_dir.reshape(2, ll, lb * W_GRP), u_t, att, f, *mix_out_w,
                         batch=lb, seq=ll, tm=tm_lat, cond_base=1, per_batch_cond=True)
        h_lat = _ffn(h_lat, mod, *ffn2, mod_base=6, tm=tm_lat, tiles_per_cond=tpc, cond_base=1, final_norm=last)

    y_prompt = h_ctx.reshape(cb, cl, D_MODEL)
    y_sample = h_lat.reshape(lb, ll, D_MODEL)
    return (y_prompt, y_sample, jnp.stack(swa_list, axis=1), jnp.stack(ax_list, axis=1),
            jnp.stack(ssm_list, axis=1))
```

```python
import functools
import math

import jax
import jax.numpy as jnp
from jax import lax
from jax.experimental import pallas as pl
from jax.experimental.pallas import tpu as pltpu

F32 = jnp.float32
BF16 = jnp.bfloat16

D_MODEL = 1024
D_FF = 2816
N_MOD = 9
EPS = 1e-6
HEAD_DIM = 64
HALF_HEAD = HEAD_DIM // 2
N_Q_HEADS = 4
Q_PER_KV = 2
W_GRP = 256
KV_W = 128
P_IN = 1536
N_SSM_GROUPS = 16
SSM_GROUP = 16
SSM_STATE = 64
N_STATE = N_SSM_GROUPS * SSM_STATE
WINDOW = 128
GRID_W = 64
ROPE_BASE = 10000.0
NEG_BIG = -1e30
LOG2E = math.log2(math.e)
Q_HEAD_ORDER = (0, 2, 1, 3)

FF_CHUNKS = (1024, 1024, 768)
SUBLANES = 8
DFT_RADIX = 64
VMEM_LIMIT = 56 * 1024 * 1024


def _cparams(sem):
    return pltpu.CompilerParams(dimension_semantics=sem, vmem_limit_bytes=VMEM_LIMIT)


def _adaln(h, g, sc, sh):
    ms = jnp.mean(h * h, axis=-1, keepdims=True)
    return (h * lax.rsqrt(ms + EPS) * g) * (1.0 + sc) + sh


def _mod_kernel(c_ref, w_ref, b_ref, o_ref):
    c = c_ref[...]
    a = (c * jax.nn.sigmoid(c)).astype(BF16)
    o_ref[...] = jnp.dot(a, w_ref[...].astype(BF16), preferred_element_type=F32) + b_ref[...]


def _modulation(cond, w_mod, b_mod):
    depth = w_mod.shape[0]
    n = N_MOD * D_MODEL
    tn = 1024
    return pl.pallas_call(
        _mod_kernel,
        grid=(depth, n // tn),
        in_specs=[
            pl.BlockSpec((16, D_MODEL), lambda l, j: (0, 0)),
            pl.BlockSpec((None, D_MODEL, tn), lambda l, j: (l, 0, j)),
            pl.BlockSpec((None, 1, tn), lambda l, j: (l, 0, j)),
        ],
        out_specs=pl.BlockSpec((None, 16, tn), lambda l, j: (l, 0, j)),
        out_shape=jax.ShapeDtypeStruct((depth, 16, n), F32),
        compiler_params=_cparams(("parallel", "parallel")),
        name="modulation",
    )(cond, w_mod, b_mod)


def _ffn_kernel(h_ref, mod_ref, g_ref, wg_ref, wu_ref, wd_ref, fg_ref, o_ref, *, mod_base, final_norm):
    h = h_ref[...]
    x = _adaln(h, g_ref[...], mod_ref[0, mod_base + 1:mod_base + 2, :],
               mod_ref[0, mod_base:mod_base + 1, :]).astype(BF16)
    acc = None
    c0 = 0
    for width in FF_CHUNKS:
        gate = jnp.dot(x, wg_ref[:, c0:c0 + width], preferred_element_type=F32)
        up = jnp.dot(x, wu_ref[:, c0:c0 + width], preferred_element_type=F32)
        hg = 0.5 * gate
        a = (hg * (1.0 + jnp.tanh(hg)) * up).astype(BF16)
        part = jnp.dot(a, wd_ref[c0:c0 + width, :], preferred_element_type=F32)
        acc = part if acc is None else acc + part
        c0 += width
    hn = h + (0.5 * mod_ref[0, mod_base + 2:mod_base + 3, :]) * acc
    if final_norm:
        ms = jnp.mean(hn * hn, axis=-1, keepdims=True)
        hn = hn * lax.rsqrt(ms + EPS) * fg_ref[...]
    o_ref[...] = hn


def _ffn(h, mod, g, wg, wu, wd, fg, *, mod_base, tm, tiles_per_cond, cond_base, final_norm):
    t = h.shape[0]
    cond = lambda i: (cond_base + i // tiles_per_cond, 0, 0)
    const = lambda i: (0, 0)
    resident = dict(pipeline_mode=pl.Buffered(1))
    kern = functools.partial(_ffn_kernel, mod_base=mod_base, final_norm=final_norm)
    return pl.pallas_call(
        kern,
        grid=(t // tm,),
        in_specs=[
            pl.BlockSpec((tm, D_MODEL), lambda i: (i, 0)),
            pl.BlockSpec((1, N_MOD, D_MODEL), cond),
            pl.BlockSpec((1, D_MODEL), const),
            pl.BlockSpec((D_MODEL, D_FF), const, **resident),
            pl.BlockSpec((D_MODEL, D_FF), const, **resident),
            pl.BlockSpec((D_FF, D_MODEL), const, **resident),
            pl.BlockSpec((1, D_MODEL), const),
        ],
        out_specs=pl.BlockSpec((tm, D_MODEL), lambda i: (i, 0)),
        out_shape=jax.ShapeDtypeStruct((t, D_MODEL), F32),
        compiler_params=_cparams(("parallel",)),
        name="ffn",
    )(h, mod, g, wg, wu, wd, fg)


def _head_mean_sq(x, e):
    x2 = x * x
    hi = x2.astype(BF16)
    lo = (x2 - hi.astype(F32)).astype(BF16)
    return jnp.dot(hi, e, preferred_element_type=F32) + jnp.dot(lo, e, preferred_element_type=F32)


def _swap_halves(x):
    w = x.shape[-1]
    lane = lax.broadcasted_iota(jnp.int32, x.shape, 1)
    first = (lane & HALF_HEAD) == 0
    return jnp.where(first, pltpu.roll(x, w - HALF_HEAD, 1), pltpu.roll(x, HALF_HEAD, 1))


def _mix_in_kernel(*refs, rope):
    if rope:
        (h_ref, mod_ref, g_ref, w_ref, qn_ref, kn_ref, e_ref, cs_ref, cos_ref, sin_ref,
         u_ref, qkv_ref, pc_ref, ps_ref, vt_ref) = refs
    else:
        (h_ref, mod_ref, g_ref, w_ref, qn_ref, kn_ref, e_ref, cs_ref,
         u_ref, qkv_ref, pc_ref, ps_ref, kv_ref) = refs
    x = _adaln(h_ref[...], g_ref[...], mod_ref[0, 4:5, :], mod_ref[0, 3:4, :]).astype(BF16)
    proj = jnp.dot(x, w_ref[...], preferred_element_type=F32)
    u_ssm = proj[:, 0:256]
    q_s = proj[:, 256:512]
    k_s = proj[:, 512:640]
    v_s = proj[:, 640:768]
    q_a = proj[:, 768:1024]
    k_a = proj[:, 1024:1152]
    v_a = proj[:, 1152:1280]
    u_f = proj[:, 1280:1536]

    e = e_ref[...]
    q_a = q_a * lax.rsqrt(_head_mean_sq(q_a, e) + EPS) * qn_ref[...]
    k_a = k_a * lax.rsqrt(_head_mean_sq(k_a, e[0:KV_W, 0:KV_W]) + EPS) * kn_ref[...]

    u_ref[...] = u_ssm
    p = jnp.dot(u_f.astype(BF16), cs_ref[...], preferred_element_type=F32)
    pc_ref[...] = p[:, 0:256].astype(BF16)
    ps_ref[...] = p[:, 256:512].astype(BF16)

    if rope:
        scale = LOG2E * HEAD_DIM ** -0.5
        cos = cos_ref[...]
        sin = sin_ref[...]
        rot = lambda t, w: t * cos[:, 0:w] + _swap_halves(t) * sin[:, 0:w]
        qkv_ref[:, 0:256] = (q_s * scale).astype(BF16)
        qkv_ref[:, 256:512] = (rot(q_s, 256) * scale).astype(BF16)
        qkv_ref[:, 512:768] = (q_a * scale).astype(BF16)
        qkv_ref[:, 768:1024] = (rot(q_a, 256) * scale).astype(BF16)
        qkv_ref[:, 1024:1152] = rot(k_s, KV_W).astype(BF16)
        qkv_ref[:, 1152:1280] = rot(k_a, KV_W).astype(BF16)
        ones = jnp.ones((HEAD_DIM, v_s.shape[0]), BF16)
        for grp, v in enumerate((v_s, v_a)):
            vt = jnp.transpose(v).astype(BF16)
            for kv in range(KV_W // HEAD_DIM):
                r0 = grp * 256 + kv * 128
                vt_ref[r0:r0 + HEAD_DIM, :] = vt[kv * HEAD_DIM:(kv + 1) * HEAD_DIM, :]
                vt_ref[r0 + HEAD_DIM:r0 + 2 * HEAD_DIM, :] = ones
    else:
        scale = HEAD_DIM ** -0.5
        qkv_ref[:, 0:256] = (q_s * scale).astype(BF16)
        qkv_ref[:, 256:384] = k_s.astype(BF16)
        qkv_ref[:, 384:512] = v_s.astype(BF16)
        qkv_ref[:, 512:768] = (q_a * scale).astype(BF16)
        qkv_ref[:, 768:896] = k_a.astype(BF16)
        qkv_ref[:, 896:1024] = v_a.astype(BF16)
        kv_ref[0] = k_s
        kv_ref[1] = v_s
        kv_ref[2] = k_a
        kv_ref[3] = v_a


def _mix_in(h, mod, g, w_in, qn, kn, e, cs, rope_tabs, *, batch, seq, tm, cond_base, rope):
    t = batch * seq
    tiles = seq // tm
    nc = batch * W_GRP
    cond = lambda i: (cond_base + (i // tiles if rope else 0), 0, 0)
    tcol = lambda i: (i % tiles, i // tiles)
    const = lambda i: (0, 0)
    in_specs = [
        pl.BlockSpec((tm, D_MODEL), lambda i: (i, 0)),
        pl.BlockSpec((1, N_MOD, D_MODEL), cond),
        pl.BlockSpec((1, D_MODEL), const),
        pl.BlockSpec((D_MODEL, P_IN), const),
        pl.BlockSpec((1, 256), const),
        pl.BlockSpec((1, KV_W), const),
        pl.BlockSpec((256, 256), const),
        pl.BlockSpec((256, 512), const),
    ]
    args = [h, mod, g, w_in, qn, kn, e, cs]
    qkv_w = 1280 if rope else 1024
    out_specs = [
        pl.BlockSpec((tm, W_GRP), tcol),
        pl.BlockSpec((tm, qkv_w), lambda i: (i, 0)),
        pl.BlockSpec((tm, W_GRP), tcol),
        pl.BlockSpec((tm, W_GRP), tcol),
    ]
    out_shape = [
        jax.ShapeDtypeStruct((seq, nc), F32),
        jax.ShapeDtypeStruct((t, qkv_w), BF16),
        jax.ShapeDtypeStruct((seq, nc), BF16),
        jax.ShapeDtypeStruct((seq, nc), BF16),
    ]
    if rope:
        in_specs += [pl.BlockSpec((tm, 256), lambda i: (i % tiles, 0))] * 2
        args += list(rope_tabs)
        out_specs.append(pl.BlockSpec((None, 512, tm), lambda i: (i // tiles, 0, i % tiles)))
        out_shape.append(jax.ShapeDtypeStruct((batch, 512, seq), BF16))
    else:
        assert tm == seq
        out_specs.append(pl.BlockSpec((None, 4, seq, KV_W), lambda i: (i, 0, 0, 0)))
        out_shape.append(jax.ShapeDtypeStruct((batch, 4, seq, KV_W), F32))
    return pl.pallas_call(
        functools.partial(_mix_in_kernel, rope=rope),
        grid=(t // tm,),
        in_specs=in_specs,
        out_specs=out_specs,
        out_shape=out_shape,
        compiler_params=_cparams(("parallel",)),
        name="mix_in",
    )(*args)


def _qk(a, b):
    return lax.dot_general(a, b, (((1,), (1,)), ((), ())), preferred_element_type=F32)


def _attn_ctx_kernel(sink_ref, qkv_ref, o_ref):
    for grp in range(2):
        base = grp * 512
        for pos, h in enumerate(Q_HEAD_ORDER):
            kv = h // Q_PER_KV
            q = qkv_ref[:, base + pos * HEAD_DIM:base + (pos + 1) * HEAD_DIM]
            k = qkv_ref[:, base + 256 + kv * HEAD_DIM:base + 256 + (kv + 1) * HEAD_DIM]
            v = qkv_ref[:, base + 384 + kv * HEAD_DIM:base + 384 + (kv + 1) * HEAD_DIM]
            s = _qk(q, k)
            m = jnp.max(s, axis=-1, keepdims=True)
            if grp == 0:
                m = jnp.maximum(m, sink_ref[h])
            p = jnp.exp(s - m)
            l = jnp.sum(p, axis=-1, keepdims=True)
            if grp == 0:
                l = l + jnp.exp(sink_ref[h] - m)
            o = jnp.dot(p.astype(BF16), v, preferred_element_type=F32) / l
            o_ref[:, grp * 256 + h * HEAD_DIM:grp * 256 + (h + 1) * HEAD_DIM] = o.astype(BF16)


def _attn_ctx(sink, qkv, *, batch, seq):
    return pl.pallas_call(
        _attn_ctx_kernel,
        grid=(batch,),
        in_specs=[
            pl.BlockSpec(memory_space=pltpu.SMEM),
            pl.BlockSpec((seq, 1024), lambda b: (b, 0)),
        ],
        out_specs=pl.BlockSpec((seq, 512), lambda b: (b, 0)),
        out_shape=jax.ShapeDtypeStruct((batch * seq, 512), BF16),
        compiler_params=_cparams(("parallel",)),
        name="attn_ctx",
    )(sink, qkv)


def _attn_lat_kernel(sink_ref, q_ref, k_ref, vt_ref, kcs_ref, vcs_ref, kca_ref, vca_ref, o_ref,
                     qm_scr, m_scr, acc_scr, s_scr, *, tq, seq, kc):
    i = pl.program_id(1)
    q0 = i * tq
    win = tq + 2 * WINDOW
    ws = pl.multiple_of(jnp.clip(q0 - WINDOW, 0, seq - win), WINDOW)
    kpos = ws + lax.broadcasted_iota(jnp.int32, (win, Q_PER_KV * tq), 0)
    qpos = q0 + (lax.broadcasted_iota(jnp.int32, (win, Q_PER_KV * tq), 1) & (tq - 1))
    in_band = jnp.abs(qpos - kpos) <= WINDOW
    lane = lax.broadcasted_iota(jnp.int32, (Q_PER_KV * tq, KV_W), 1)
    n_chunks = seq // kc
    n_kv = N_Q_HEADS // Q_PER_KV

    def update(kv, s, vt):
        m_old = m_scr[kv]
        m_new = jnp.maximum(m_old, jnp.max(s, axis=0, keepdims=True))
        p = jnp.exp2(s - m_new).astype(BF16)
        acc_scr[kv] = (jnp.exp2(m_old - m_new) * acc_scr[kv]
                       + jnp.dot(vt, p, preferred_element_type=F32))
        m_scr[kv] = m_new

    def sink_row(kv):
        return jnp.concatenate(
            [jnp.full((1, tq), sink_ref[Q_PER_KV * kv + j] * LOG2E, F32) for j in range(Q_PER_KV)], axis=1)

    for grp in range(2):
        acc_scr[...] = jnp.zeros(acc_scr.shape, F32)
        base = grp * 512
        q_plain = jnp.concatenate([q_ref[:, base:base + KV_W], q_ref[:, base + KV_W:base + 2 * KV_W]], axis=0)
        q_rope = jnp.concatenate([q_ref[:, base + 256:base + 256 + KV_W],
                                  q_ref[:, base + 256 + KV_W:base + 256 + 2 * KV_W]], axis=0)
        k_ctx = kcs_ref[...] if grp == 0 else kca_ref[...]
        vc_ref = vcs_ref if grp == 0 else vca_ref
        s_ctx = []
        for kv in range(n_kv):
            keep = (lane < HEAD_DIM) if kv == 0 else (lane >= HEAD_DIM)
            qm_scr[kv] = jnp.where(keep, q_rope, 0)
            s_ctx.append(_qk(k_ctx, jnp.where(keep, q_plain, 0)))
            m_scr[kv] = sink_row(kv) if grp == 0 else jnp.full((1, Q_PER_KV * tq), NEG_BIG, F32)
        if grp == 0:
            k_win = k_ref[pl.ds(ws, win), 0:KV_W]
            s_win = [jnp.where(in_band, _qk(k_win, qm_scr[kv]), NEG_BIG) for kv in range(n_kv)]
            for kv in range(n_kv):
                update(kv, s_ctx[kv], vc_ref[kv * 128:(kv + 1) * 128, :])
            for kv in range(n_kv):
                update(kv, s_win[kv], vt_ref[kv * 128:(kv + 1) * 128, pl.ds(ws, win)])
        else:
            def scores_to(slot, c):
                k_c = k_ref[pl.ds(pl.multiple_of(c * kc, kc), kc), KV_W:2 * KV_W]
                for kv in range(n_kv):
                    s_scr[slot, kv] = _qk(k_c, qm_scr[kv])

            def update_from(slot, c):
                cols = pl.ds(pl.multiple_of(c * kc, kc), kc)
                for kv in range(n_kv):
                    update(kv, s_scr[slot, kv], vt_ref[256 + kv * 128:256 + (kv + 1) * 128, cols])

            def chunk_pair(j, carry):
                c = 2 * j
                scores_to(1, c + 1)
                update_from(0, c)
                scores_to(0, c + 2)
                update_from(1, c + 1)
                return carry

            scores_to(0, 0)
            for kv in range(n_kv):
                update(kv, s_ctx[kv], vc_ref[kv * 128:(kv + 1) * 128, :])
            lax.fori_loop(0, n_chunks // 2 - 1, chunk_pair, 0)
            scores_to(1, n_chunks - 1)
            update_from(0, n_chunks - 2)
            update_from(1, n_chunks - 1)
        outs = []
        for kv in range(n_kv):
            acc = acc_scr[kv]
            den = acc[HEAD_DIM:HEAD_DIM + 1, :]
            if grp == 0:
                den = den + jnp.exp2(sink_row(kv) - m_scr[kv])
            o = acc[0:HEAD_DIM, :] / den
            outs += [o[:, j * tq:(j + 1) * tq] for j in range(Q_PER_KV)]
        o_t = jnp.concatenate(outs, axis=0)
        o_ref[:, grp * 256:(grp + 1) * 256] = jnp.transpose(o_t).astype(BF16)


def _attn_lat(sink, qkv, vt, kc_s, vc_s, kc_a, vc_a, *, batch, seq, tq, kc=512):
    nq = seq // tq
    past = kc_s.shape[1]
    per_b3 = lambda b, i: (b, 0, 0)
    return pl.pallas_call(
        functools.partial(_attn_lat_kernel, tq=tq, seq=seq, kc=kc),
        grid=(batch, nq),
        in_specs=[
            pl.BlockSpec(memory_space=pltpu.SMEM),
            pl.BlockSpec((tq, 1024), lambda b, i: (b * nq + i, 0)),
            pl.BlockSpec((seq, 256), lambda b, i: (b, 4)),
            pl.BlockSpec((None, 512, seq), per_b3),
            pl.BlockSpec((None, past, KV_W), per_b3),
            pl.BlockSpec((None, 256, past), per_b3),
            pl.BlockSpec((None, past, KV_W), per_b3),
            pl.BlockSpec((None, 256, past), per_b3),
        ],
        out_specs=pl.BlockSpec((tq, 512), lambda b, i: (b * nq + i, 0)),
        out_shape=jax.ShapeDtypeStruct((batch * seq, 512), BF16),
        scratch_shapes=[
            pltpu.VMEM((N_Q_HEADS // Q_PER_KV, Q_PER_KV * tq, KV_W), BF16),
            pltpu.VMEM((N_Q_HEADS // Q_PER_KV, 1, Q_PER_KV * tq), F32),
            pltpu.VMEM((N_Q_HEADS // Q_PER_KV, 128, Q_PER_KV * tq), F32),
            pltpu.VMEM((2, N_Q_HEADS // Q_PER_KV, kc, Q_PER_KV * tq), F32),
        ],
        compiler_params=_cparams(("parallel", "parallel")),
        name="attn_lat",
    )(sink, qkv, qkv, vt, kc_s, vc_s, kc_a, vc_a)


def _ssm_kernel(u_ref, bm_ref, cm_ref, a_ref, s0_ref, y_ref, fin_ref, bu_scr, st_scr, *, lc):
    d = pl.program_id(0)
    n = pl.program_id(2)

    @pl.when(n == 0)
    def _():
        st_scr[...] = s0_ref[...]

    u = u_ref[...].reshape(lc * SUBLANES, W_GRP).astype(BF16)
    bu_scr[...] = jnp.dot(u, bm_ref[...], preferred_element_type=F32)
    ar = jnp.broadcast_to(a_ref[0:1, :], (SUBLANES, N_STATE))
    ai = jnp.broadcast_to(a_ref[1:2, :], (SUBLANES, N_STATE))

    def step(t, carry):
        sr, si = carry
        tt = jnp.where(d == 0, t, lc - 1 - t)
        r0 = pl.multiple_of(tt * SUBLANES, SUBLANES)
        br = bu_scr[pl.ds(r0, SUBLANES), 0:N_STATE]
        bi = bu_scr[pl.ds(r0, SUBLANES), N_STATE:2 * N_STATE]
        nr = ar * sr - ai * si + br
        ni = ar * si + ai * sr + bi
        bu_scr[pl.ds(r0, SUBLANES), 0:N_STATE] = nr
        bu_scr[pl.ds(r0, SUBLANES), N_STATE:2 * N_STATE] = ni
        return nr, ni

    sr, si = lax.fori_loop(0, lc, step, (st_scr[:, 0:N_STATE], st_scr[:, N_STATE:2 * N_STATE]), unroll=8)
    st_scr[:, 0:N_STATE] = sr
    st_scr[:, N_STATE:2 * N_STATE] = si
    y = jnp.dot(bu_scr[...].astype(BF16), cm_ref[...], preferred_element_type=F32)
    y_ref[...] = y.reshape(lc, SUBLANES, W_GRP)

    @pl.when(n == pl.num_programs(2) - 1)
    def _():
        fin_ref[...] = st_scr[...]


def _ssm(u_t, bmat, cmat, a, s0, *, batch, seq, lc):
    nch = seq // lc
    chunk = lambda d, n: n + d * (nch - 1 - 2 * n)
    return pl.pallas_call(
        functools.partial(_ssm_kernel, lc=lc),
        grid=(2, batch // SUBLANES, nch),
        in_specs=[
            pl.BlockSpec((lc, SUBLANES, W_GRP), lambda d, b, n: (chunk(d, n), b, 0)),
            pl.BlockSpec((None, W_GRP, 2 * N_STATE), lambda d, b, n: (d, 0, 0)),
            pl.BlockSpec((None, 2 * N_STATE, W_GRP), lambda d, b, n: (d, 0, 0)),
            pl.BlockSpec((None, 2, N_STATE), lambda d, b, n: (d, 0, 0)),
            pl.BlockSpec((None, SUBLANES, 2 * N_STATE), lambda d, b, n: (d, b, 0)),
        ],
        out_specs=[
            pl.BlockSpec((None, lc, SUBLANES, W_GRP), lambda d, b, n: (d, chunk(d, n), b, 0)),
            pl.BlockSpec((None, SUBLANES, 2 * N_STATE), lambda d, b, n: (d, b, 0)),
        ],
        out_shape=[
            jax.ShapeDtypeStruct((2, seq, batch, W_GRP), F32),
            jax.ShapeDtypeStruct((2, batch, 2 * N_STATE), F32),
        ],
        scratch_shapes=[
            pltpu.VMEM((lc * SUBLANES, 2 * N_STATE), F32),
            pltpu.VMEM((SUBLANES, 2 * N_STATE), F32),
        ],
        compiler_params=_cparams(("parallel", "parallel", "arbitrary")),
        name="ssm_scan",
    )(u_t, bmat, cmat, a, s0)


def _dft_kernel(t1c_ref, t1s_ref, t2c_ref, t2s_ref, pc_ref, ps_ref, o_ref, lc_scr, ls_scr, acc_scr,
                *, tm, scale):
    kb = pl.program_id(2)
    t2c = t2c_ref[...]
    t2s = t2s_ref[...]
    for a in range(tm // DFT_RADIX):
        c1 = t1c_ref[a:a + 1, :]
        s1 = t1s_ref[a:a + 1, :]
        rows = slice(a * DFT_RADIX, (a + 1) * DFT_RADIX)
        lc_scr[rows, :] = (c1 * t2c - s1 * t2s).astype(BF16)
        ls_scr[rows, :] = (s1 * t2c + c1 * t2s).astype(BF16)
    part = (jnp.dot(lc_scr[...], pc_ref[...], preferred_element_type=F32)
            - jnp.dot(ls_scr[...], ps_ref[...], preferred_element_type=F32))

    @pl.when(kb == 0)
    def _():
        acc_scr[...] = part

    @pl.when(kb > 0)
    def _():
        acc_scr[...] += part

    @pl.when(kb == pl.num_programs(2) - 1)
    def _():
        o_ref[...] = (acc_scr[...] * scale).astype(BF16)


def _dft(tabs, pc, ps, *, seq, tm, tk, tn):
    nc = pc.shape[1]
    t1c, t1s, t2c, t2s = tabs
    ra = tm // DFT_RADIX
    scale = 1.0 / math.sqrt(seq * W_GRP)
    return pl.pallas_call(
        functools.partial(_dft_kernel, tm=tm, scale=scale),
        grid=(seq // tm, nc // tn, seq // tk),
        in_specs=[
            pl.BlockSpec((ra, tk), lambda i, j, k: (i, k)),
            pl.BlockSpec((ra, tk), lambda i, j, k: (i, k)),
            pl.BlockSpec((DFT_RADIX, tk), lambda i, j, k: (0, k)),
            pl.BlockSpec((DFT_RADIX, tk), lambda i, j, k: (0, k)),
            pl.BlockSpec((tk, tn), lambda i, j, k: (k, j)),
            pl.BlockSpec((tk, tn), lambda i, j, k: (k, j)),
        ],
        out_specs=pl.BlockSpec((tm, tn), lambda i, j, k: (i, j)),
        out_shape=jax.ShapeDtypeStruct((seq, nc), BF16),
        scratch_shapes=[pltpu.VMEM((tm, tk), BF16), pltpu.VMEM((tm, tk), BF16), pltpu.VMEM((tm, tn), F32)],
        compiler_params=_cparams(("parallel", "parallel", "arbitrary")),
        name="pos_dft",
    )(t1c, t1s, t2c, t2s, pc, ps)


def _mix_out_kernel(h_ref, mod_ref, yf_ref, yb_ref, u_ref, att_ref, f_ref, d_ref, wglu_ref, bglu_ref,
                    wf_ref, bf_ref, wo_ref, o_ref, m_scr):
    y = d_ref[...] * u_ref[...] + yf_ref[...] + yb_ref[...]
    y = y * (0.5 * (1.0 + jnp.tanh(math.sqrt(2.0 / math.pi) * (y + 0.044715 * (y * y * y)))))
    z = jnp.dot(y.astype(BF16), wglu_ref[...], preferred_element_type=F32) + bglu_ref[...]
    m_scr[:, 0:256] = (y * jax.nn.sigmoid(z)).astype(BF16)
    m_scr[:, 256:768] = att_ref[...]
    m_scr[:, 768:1024] = (jnp.dot(f_ref[...], wf_ref[...], preferred_element_type=F32) + bf_ref[...]).astype(BF16)
    mixed = jnp.dot(m_scr[...], wo_ref[...], preferred_element_type=F32)
    o_ref[...] = h_ref[...] + mod_ref[0, 5:6, :] * mixed


def _mix_out(h, mod, y_dir, u_t, att, f, d, wglu, bglu, wf, bfn, wo, *, batch, seq, tm, cond_base, per_batch_cond):
    t = batch * seq
    tiles = seq // tm
    cond = lambda i: (cond_base + (i // tiles if per_batch_cond else 0), 0, 0)
    tcol = lambda i: (i % tiles, i // tiles)
    const = lambda i: (0, 0)
    return pl.pallas_call(
        _mix_out_kernel,
        grid=(t // tm,),
        in_specs=[
            pl.BlockSpec((tm, D_MODEL), lambda i: (i, 0)),
            pl.BlockSpec((1, N_MOD, D_MODEL), cond),
            pl.BlockSpec((None, tm, W_GRP), lambda i: (0, i % tiles, i // tiles)),
            pl.BlockSpec((None, tm, W_GRP), lambda i: (1, i % tiles, i // tiles)),
            pl.BlockSpec((tm, W_GRP), tcol),
            pl.BlockSpec((tm, 512), lambda i: (i, 0)),
            pl.BlockSpec((tm, W_GRP), tcol),
            pl.BlockSpec((1, W_GRP), const),
            pl.BlockSpec((W_GRP, W_GRP), const),
            pl.BlockSpec((1, W_GRP), const),
            pl.BlockSpec((W_GRP, W_GRP), const),
            pl.BlockSpec((1, W_GRP), const),
            pl.BlockSpec((D_MODEL, D_MODEL), const),
        ],
        out_specs=pl.BlockSpec((tm, D_MODEL), lambda i: (i, 0)),
        out_shape=jax.ShapeDtypeStruct((t, D_MODEL), F32),
        scratch_shapes=[pltpu.VMEM((tm, D_MODEL), BF16)],
        compiler_params=_cparams(("parallel",)),
        name="mix_out",
    )(h, mod, y_dir, y_dir, u_t, att, f, d, wglu, bglu, wf, bfn, wo)


def _rope_tables(seq):
    rows = seq // GRID_W
    row_id = jnp.repeat(jnp.arange(rows), GRID_W).astype(F32)
    col_id = jnp.tile(jnp.arange(GRID_W), rows).astype(F32)
    n_freq = HEAD_DIM // 4
    inv = ROPE_BASE ** (-jnp.arange(n_freq, dtype=F32) / n_freq)
    ang = jnp.concatenate([row_id[:, None] * inv, col_id[:, None] * inv], axis=-1)
    cos, sin = jnp.cos(ang), jnp.sin(ang)
    cos_full = jnp.tile(jnp.concatenate([cos, cos], axis=-1), (1, N_Q_HEADS))
    sin_signed = jnp.tile(jnp.concatenate([-sin, sin], axis=-1), (1, N_Q_HEADS))
    return cos_full, sin_signed


def _angle_table(mult, n, period):
    m = (mult[:, None] * n[None, :]) % period
    th = m.astype(F32) * (2.0 * math.pi / period)
    return jnp.cos(th), jnp.sin(th)


def _dft_tables(seq):
    n = jnp.arange(seq, dtype=jnp.int32)
    t1c, t1s = _angle_table(DFT_RADIX * jnp.arange(seq // DFT_RADIX, dtype=jnp.int32), n, seq)
    t2c, t2s = _angle_table(jnp.arange(DFT_RADIX, dtype=jnp.int32), n, seq)
    return t1c, t1s, t2c, t2s


def _ssm_params(lam_re, lam_im, b_re, b_im, c_re, c_im, log_dt):
    dt = jnp.exp(log_dt)[..., None]
    mag = jnp.exp(lam_re * dt)
    ar = mag * jnp.cos(lam_im * dt)
    ai = mag * jnp.sin(lam_im * dt)
    den = lam_re * lam_re + lam_im * lam_im
    qr = ((ar - 1.0) * lam_re + ai * lam_im) / den
    qi = (ai * lam_re - (ar - 1.0) * lam_im) / den
    bb_re = qr[..., None] * b_re - qi[..., None] * b_im
    bb_im = qr[..., None] * b_im + qi[..., None] * b_re
    eye = jnp.eye(N_SSM_GROUPS, dtype=F32)

    def block_diag(x):
        y = jnp.transpose(x, (0, 1, 3, 2))[:, :, :, None, :] * eye[None, :, None, :, None]
        return y.reshape(2, x.shape[1] * x.shape[3], x.shape[1] * x.shape[2])

    bmat = jnp.concatenate([block_diag(bb_re), block_diag(bb_im)], axis=-1).astype(BF16)
    cmat = jnp.concatenate([block_diag(c_re), block_diag(-c_im)], axis=1).astype(BF16)
    a2 = jnp.stack([ar.reshape(2, N_STATE), ai.reshape(2, N_STATE)], axis=1)
    return a2, bmat, cmat


def _permute_q_heads(w_in):
    cols = jnp.arange(P_IN, dtype=jnp.int32)
    for base in (256, 768):
        blk = jnp.concatenate([base + h * HEAD_DIM + jnp.arange(HEAD_DIM, dtype=jnp.int32) for h in Q_HEAD_ORDER])
        cols = cols.at[base:base + N_Q_HEADS * HEAD_DIM].set(blk)
    return jnp.take(w_in, cols, axis=1)


def _cache_layout(cache):
    b, _, past, nkv, dh = cache.shape
    keys = cache[:, 0].reshape(b, past, nkv * dh).astype(BF16)
    vt = jnp.transpose(cache[:, 1], (0, 2, 3, 1))
    vt = jnp.concatenate([vt, jnp.ones_like(vt)], axis=2).reshape(b, 2 * nkv * dh, past).astype(BF16)
    return keys, vt


def kernel(x_prompt, x_sample, cache_swa_kv, cache_axial_kv, state_ssm, c, c_ctx, w_mod, b_mod, norm_ffn1, norm_mix, norm_ffn2, ffn1_w_gate, ffn1_w_up, ffn1_w_down, ffn2_w_gate, ffn2_w_up, ffn2_w_down, w_in, w_out, ssm_lambda_re, ssm_lambda_im, ssm_b_re, ssm_b_im, ssm_c_re, ssm_c_im, ssm_log_dt, ssm_d, ssm_w_glu, ssm_b_glu, swa_sink, ax_q_norm, ax_k_norm, fnet_w, fnet_b, final_norm):
    depth = w_mod.shape[0]
    cb, cl, _ = x_prompt.shape
    lb, ll, _ = x_sample.shape

    cond = jnp.zeros((16, D_MODEL), F32).at[0].set(c_ctx).at[1:1 + lb].set(c)
    mod_all = _modulation(cond, w_mod, b_mod.reshape(depth, 1, N_MOD * D_MODEL))
    mod_all = mod_all.reshape(depth, 16, N_MOD, D_MODEL)

    e_heads = jnp.kron(jnp.eye(256 // HEAD_DIM, dtype=F32), jnp.ones((HEAD_DIM, HEAD_DIM), F32)) / HEAD_DIM
    e_heads = e_heads.astype(BF16)
    kc = jnp.arange(W_GRP, dtype=jnp.int32)
    cc, sc = _angle_table(kc, kc, W_GRP)
    cs_chan = jnp.concatenate([cc, sc], axis=-1).astype(BF16)
    rope_tabs = _rope_tables(ll)
    dft_ctx = _dft_tables(cl)
    dft_lat = _dft_tables(ll)
    fg = final_norm.reshape(1, D_MODEL)

    h_ctx = x_prompt.reshape(cb * cl, D_MODEL)
    h_lat = x_sample.reshape(lb * ll, D_MODEL)
    tm_lat = 512
    swa_list, ax_list, ssm_list = [], [], []
    for l in range(depth):
        mod = mod_all[l]
        bf = lambda w: w[l].astype(BF16)
        row = lambda v: v[l].reshape(1, -1)
        ffn1 = (row(norm_ffn1), bf(ffn1_w_gate), bf(ffn1_w_up), bf(ffn1_w_down), fg)
        ffn2 = (row(norm_ffn2), bf(ffn2_w_gate), bf(ffn2_w_up), bf(ffn2_w_down), fg)
        w_in_l, w_out_l = _permute_q_heads(w_in[l]).astype(BF16), bf(w_out)
        qn = jnp.tile(ax_q_norm[l], N_Q_HEADS).reshape(1, 256)
        kn = jnp.tile(ax_k_norm[l], KV_W // HEAD_DIM).reshape(1, KV_W)
        a2, bmat, cmat = _ssm_params(ssm_lambda_re[l], ssm_lambda_im[l], ssm_b_re[l], ssm_b_im[l],
                                     ssm_c_re[l], ssm_c_im[l], ssm_log_dt[l])
        mix_out_w = (row(ssm_d), bf(ssm_w_glu), row(ssm_b_glu), bf(fnet_w), row(fnet_b), w_out_l)
        sink = swa_sink[l]
        last = l == depth - 1

        h_ctx = _ffn(h_ctx, mod, *ffn1, mod_base=0, tm=512, tiles_per_cond=1 << 30, cond_base=0, final_norm=False)
        u_t, qkv, pc, ps, kv_new = _mix_in(h_ctx, mod, row(norm_mix), w_in_l, qn, kn, e_heads, cs_chan, None,
                                           batch=cb, seq=cl, tm=cl, cond_base=0, rope=False)
        y_dir, s_fin = _ssm(u_t.reshape(cl, cb, W_GRP), bmat, cmat, a2,
                            jnp.zeros((2, cb, 2 * N_STATE), F32), batch=cb, seq=cl, lc=128)
        att = _attn_ctx(sink, qkv, batch=cb, seq=cl)
        f = _dft(dft_ctx, pc, ps, seq=cl, tm=cl, tk=cl, tn=2048)
        h_ctx = _mix_out(h_ctx, mod, y_dir.reshape(2, cl, cb * W_GRP), u_t, att, f, *mix_out_w,
                         batch=cb, seq=cl, tm=cl, cond_base=0, per_batch_cond=False)
        h_ctx = _ffn(h_ctx, mod, *ffn2, mod_base=6, tm=512, tiles_per_cond=1 << 30, cond_base=0, final_norm=last)
        swa_list.append(kv_new[:, 0:2].reshape(cb, 2, cl, KV_W // HEAD_DIM, HEAD_DIM))
        ax_list.append(kv_new[:, 2:4].reshape(cb, 2, cl, KV_W // HEAD_DIM, HEAD_DIM))
        s_fin = s_fin.reshape(2, cb, 2, N_SSM_GROUPS, SSM_STATE)
        ssm_list.append(jnp.transpose(s_fin, (1, 0, 3, 4, 2)))

        tpc = ll // tm_lat
        h_lat = _ffn(h_lat, mod, *ffn1, mod_base=0, tm=tm_lat, tiles_per_cond=tpc, cond_base=1, final_norm=False)
        u_t, qkv, pc, ps, vt = _mix_in(h_lat, mod, row(norm_mix), w_in_l, qn, kn, e_heads, cs_chan, rope_tabs,
                                       batch=lb, seq=ll, tm=tm_lat, cond_base=1, rope=True)
        s0 = jnp.transpose(state_ssm[:, l], (1, 0, 4, 2, 3)).reshape(2, lb, 2 * N_STATE)
        y_dir, _ = _ssm(u_t.reshape(ll, lb, W_GRP), bmat, cmat, a2, s0, batch=lb, seq=ll, lc=128)
        kc_s, vc_s = _cache_layout(cache_swa_kv[:, l])
        kc_a, vc_a = _cache_layout(cache_axial_kv[:, l])
        att = _attn_lat(sink, qkv, vt, kc_s, vc_s, kc_a, vc_a, batch=lb, seq=ll, tq=256)
        f = _dft(dft_lat, pc, ps, seq=ll, tm=512, tk=512, tn=2048)
        h_lat = _mix_out(h_lat, mod, y_dir.reshape(2, ll, lb * W_GRP), u_t, att, f, *mix_out_w,
                         batch=lb, seq=ll, tm=tm_lat, cond_base=1, per_batch_cond=True)
        h_lat = _ffn(h_lat, mod, *ffn2, mod_base=6, tm=tm_lat, tiles_per_cond=tpc, cond_base=1, final_norm=last)

    y_prompt = h_ctx.reshape(cb, cl, D_MODEL)
    y_sample = h_lat.reshape(lb, ll, D_MODEL)
    return (y_prompt, y_sample, jnp.stack(swa_list, axis=1), jnp.stack(ax_list, axis=1),
            jnp.stack(ssm_list, axis=1))
```

```python
import functools
import math

import jax
import jax.numpy as jnp
from jax import lax
from jax.experimental import pallas as pl
from jax.experimental.pallas import tpu as pltpu

F32 = jnp.float32
BF16 = jnp.bfloat16

D_MODEL = 1024
D_FF = 2816
N_MOD = 9
EPS = 1e-6
HEAD_DIM = 64
HALF_HEAD = HEAD_DIM // 2
N_Q_HEADS = 4
Q_PER_KV = 2
W_GRP = 256
KV_W = 128
P_IN = 1536
N_SSM_GROUPS = 16
SSM_GROUP = 16
SSM_STATE = 64
N_STATE = N_SSM_GROUPS * SSM_STATE
WINDOW = 128
GRID_W = 64
ROPE_BASE = 10000.0
NEG_BIG = -1e30
LOG2E = math.log2(math.e)
Q_HEAD_ORDER = (0, 2, 1, 3)

FF_CHUNKS = (1024, 1024, 768)
SUBLANES = 8
LANES = 128
DFT_RADIX = 64
VMEM_LIMIT = 56 * 1024 * 1024


def _cparams(sem):
    return pltpu.CompilerParams(dimension_semantics=sem, vmem_limit_bytes=VMEM_LIMIT)


def _adaln(h, g, sc, sh):
    ms = jnp.mean(h * h, axis=-1, keepdims=True)
    return (h * lax.rsqrt(ms + EPS) * g) * (1.0 + sc) + sh


def _mod_kernel(c_ref, w_ref, b_ref, o_ref):
    c = c_ref[...]
    a = (c * jax.nn.sigmoid(c)).astype(BF16)
    o_ref[...] = jnp.dot(a, w_ref[...].astype(BF16), preferred_element_type=F32) + b_ref[...]


def _modulation(cond, w_mod, b_mod):
    depth = w_mod.shape[0]
    n = N_MOD * D_MODEL
    tn = 1024
    return pl.pallas_call(
        _mod_kernel,
        grid=(depth, n // tn),
        in_specs=[
            pl.BlockSpec((16, D_MODEL), lambda l, j: (0, 0)),
            pl.BlockSpec((None, D_MODEL, tn), lambda l, j: (l, 0, j)),
            pl.BlockSpec((None, 1, tn), lambda l, j: (l, 0, j)),
        ],
        out_specs=pl.BlockSpec((None, 16, tn), lambda l, j: (l, 0, j)),
        out_shape=jax.ShapeDtypeStruct((depth, 16, n), F32),
        compiler_params=_cparams(("parallel", "parallel")),
        name="modulation",
    )(cond, w_mod, b_mod)


def _ffn_kernel(h_ref, mod_ref, g_ref, wg_ref, wu_ref, wd_ref, fg_ref, o_ref, *, mod_base, final_norm):
    h = h_ref[...]
    x = _adaln(h, g_ref[...], mod_ref[0, mod_base + 1:mod_base + 2, :],
               mod_ref[0, mod_base:mod_base + 1, :]).astype(BF16)
    acc = None
    c0 = 0
    for width in FF_CHUNKS:
        gate = jnp.dot(x, wg_ref[:, c0:c0 + width], preferred_element_type=F32)
        up = jnp.dot(x, wu_ref[:, c0:c0 + width], preferred_element_type=F32)
        hg = 0.5 * gate
        a = (hg * (1.0 + jnp.tanh(hg)) * up).astype(BF16)
        part = jnp.dot(a, wd_ref[c0:c0 + width, :], preferred_element_type=F32)
        acc = part if acc is None else acc + part
        c0 += width
    hn = h + (0.5 * mod_ref[0, mod_base + 2:mod_base + 3, :]) * acc
    if final_norm:
        ms = jnp.mean(hn * hn, axis=-1, keepdims=True)
        hn = hn * lax.rsqrt(ms + EPS) * fg_ref[...]
    o_ref[...] = hn


def _ffn(h, mod, g, wg, wu, wd, fg, *, mod_base, tm, tiles_per_cond, cond_base, final_norm):
    t = h.shape[0]
    cond = lambda i: (cond_base + i // tiles_per_cond, 0, 0)
    const = lambda i: (0, 0)
    resident = dict(pipeline_mode=pl.Buffered(1))
    kern = functools.partial(_ffn_kernel, mod_base=mod_base, final_norm=final_norm)
    return pl.pallas_call(
        kern,
        grid=(t // tm,),
        in_specs=[
            pl.BlockSpec((tm, D_MODEL), lambda i: (i, 0)),
            pl.BlockSpec((1, N_MOD, D_MODEL), cond),
            pl.BlockSpec((1, D_MODEL), const),
            pl.BlockSpec((D_MODEL, D_FF), const, **resident),
            pl.BlockSpec((D_MODEL, D_FF), const, **resident),
            pl.BlockSpec((D_FF, D_MODEL), const, **resident),
            pl.BlockSpec((1, D_MODEL), const),
        ],
        out_specs=pl.BlockSpec((tm, D_MODEL), lambda i: (i, 0)),
        out_shape=jax.ShapeDtypeStruct((t, D_MODEL), F32),
        compiler_params=_cparams(("parallel",)),
        name="ffn",
    )(h, mod, g, wg, wu, wd, fg)


def _head_mean_sq(x, e):
    x2 = x * x
    hi = x2.astype(BF16)
    lo = (x2 - hi.astype(F32)).astype(BF16)
    return jnp.dot(hi, e, preferred_element_type=F32) + jnp.dot(lo, e, preferred_element_type=F32)


def _swap_halves(x):
    w = x.shape[-1]
    lane = lax.broadcasted_iota(jnp.int32, x.shape, 1)
    first = (lane & HALF_HEAD) == 0
    return jnp.where(first, pltpu.roll(x, w - HALF_HEAD, 1), pltpu.roll(x, HALF_HEAD, 1))


def _mix_in_kernel(*refs, rope):
    if rope:
        (h_ref, mod_ref, g_ref, w_ref, qn_ref, kn_ref, e_ref, cs_ref, cos_ref, sin_ref,
         u_ref, qkv_ref, pc_ref, ps_ref, vt_ref) = refs
    else:
        (h_ref, mod_ref, g_ref, w_ref, qn_ref, kn_ref, e_ref, cs_ref,
         u_ref, qkv_ref, pc_ref, ps_ref, kvs_ref, kva_ref) = refs
    x = _adaln(h_ref[...], g_ref[...], mod_ref[0, 4:5, :], mod_ref[0, 3:4, :]).astype(BF16)
    proj = jnp.dot(x, w_ref[...], preferred_element_type=F32)
    u_ssm = proj[:, 0:256]
    q_s = proj[:, 256:512]
    k_s = proj[:, 512:640]
    v_s = proj[:, 640:768]
    q_a = proj[:, 768:1024]
    k_a = proj[:, 1024:1152]
    v_a = proj[:, 1152:1280]
    u_f = proj[:, 1280:1536]

    e = e_ref[...]
    q_a = q_a * lax.rsqrt(_head_mean_sq(q_a, e) + EPS) * qn_ref[...]
    k_a = k_a * lax.rsqrt(_head_mean_sq(k_a, e[0:KV_W, 0:KV_W]) + EPS) * kn_ref[...]

    u_ref[...] = u_ssm
    p = jnp.dot(u_f.astype(BF16), cs_ref[...], preferred_element_type=F32)
    pc_ref[...] = p[:, 0:256].astype(BF16)
    ps_ref[...] = p[:, 256:512].astype(BF16)

    if rope:
        scale = LOG2E * HEAD_DIM ** -0.5
        cos = cos_ref[...]
        sin = sin_ref[...]
        rot = lambda t, w: t * cos[:, 0:w] + _swap_halves(t) * sin[:, 0:w]
        qkv_ref[:, 0:256] = (q_s * scale).astype(BF16)
        qkv_ref[:, 256:512] = (rot(q_s, 256) * scale).astype(BF16)
        qkv_ref[:, 512:768] = (q_a * scale).astype(BF16)
        qkv_ref[:, 768:1024] = (rot(q_a, 256) * scale).astype(BF16)
        qkv_ref[:, 1024:1152] = rot(k_s, KV_W).astype(BF16)
        qkv_ref[:, 1152:1280] = rot(k_a, KV_W).astype(BF16)
        ones = jnp.ones((HEAD_DIM, v_s.shape[0]), BF16)
        for grp, v in enumerate((v_s, v_a)):
            vt = jnp.transpose(v).astype(BF16)
            for kv in range(KV_W // HEAD_DIM):
                r0 = grp * 256 + kv * 128
                vt_ref[r0:r0 + HEAD_DIM, :] = vt[kv * HEAD_DIM:(kv + 1) * HEAD_DIM, :]
                vt_ref[r0 + HEAD_DIM:r0 + 2 * HEAD_DIM, :] = ones
    else:
        scale = HEAD_DIM ** -0.5
        qkv_ref[:, 0:256] = (q_s * scale).astype(BF16)
        qkv_ref[:, 256:384] = k_s.astype(BF16)
        qkv_ref[:, 384:512] = v_s.astype(BF16)
        qkv_ref[:, 512:768] = (q_a * scale).astype(BF16)
        qkv_ref[:, 768:896] = k_a.astype(BF16)
        qkv_ref[:, 896:1024] = v_a.astype(BF16)
        kvs_ref[0] = k_s
        kvs_ref[1] = v_s
        kva_ref[0] = k_a
        kva_ref[1] = v_a


def _mix_in(h, mod, g, w_in, qn, kn, e, cs, rope_tabs, *, batch, seq, tm, cond_base, rope):
    t = batch * seq
    tiles = seq // tm
    nc = batch * W_GRP
    cond = lambda i: (cond_base + (i // tiles if rope else 0), 0, 0)
    tcol = lambda i: (i % tiles, i // tiles)
    const = lambda i: (0, 0)
    in_specs = [
        pl.BlockSpec((tm, D_MODEL), lambda i: (i, 0)),
        pl.BlockSpec((1, N_MOD, D_MODEL), cond),
        pl.BlockSpec((1, D_MODEL), const),
        pl.BlockSpec((D_MODEL, P_IN), const),
        pl.BlockSpec((1, 256), const),
        pl.BlockSpec((1, KV_W), const),
        pl.BlockSpec((256, 256), const),
        pl.BlockSpec((256, 512), const),
    ]
    args = [h, mod, g, w_in, qn, kn, e, cs]
    qkv_w = 1280 if rope else 1024
    out_specs = [
        pl.BlockSpec((tm, W_GRP), tcol),
        pl.BlockSpec((tm, qkv_w), lambda i: (i, 0)),
        pl.BlockSpec((tm, W_GRP), tcol),
        pl.BlockSpec((tm, W_GRP), tcol),
    ]
    out_shape = [
        jax.ShapeDtypeStruct((seq, nc), F32),
        jax.ShapeDtypeStruct((t, qkv_w), BF16),
        jax.ShapeDtypeStruct((seq, nc), BF16),
        jax.ShapeDtypeStruct((seq, nc), BF16),
    ]
    if rope:
        in_specs += [pl.BlockSpec((tm, 256), lambda i: (i % tiles, 0))] * 2
        args += list(rope_tabs)
        out_specs.append(pl.BlockSpec((None, 512, tm), lambda i: (i // tiles, 0, i % tiles)))
        out_shape.append(jax.ShapeDtypeStruct((batch, 512, seq), BF16))
    else:
        assert tm == seq
        out_specs += [pl.BlockSpec((None, 2, seq, KV_W), lambda i: (i, 0, 0, 0))] * 2
        out_shape += [jax.ShapeDtypeStruct((batch, 2, seq, KV_W), F32)] * 2
    return pl.pallas_call(
        functools.partial(_mix_in_kernel, rope=rope),
        grid=(t // tm,),
        in_specs=in_specs,
        out_specs=out_specs,
        out_shape=out_shape,
        compiler_params=_cparams(("parallel",)),
        name="mix_in",
    )(*args)


def _qk(a, b):
    return lax.dot_general(a, b, (((1,), (1,)), ((), ())), preferred_element_type=F32)


def _attn_ctx_kernel(sink_ref, qkv_ref, o_ref):
    for grp in range(2):
        base = grp * 512
        for pos, h in enumerate(Q_HEAD_ORDER):
            kv = h // Q_PER_KV
            q = qkv_ref[:, base + pos * HEAD_DIM:base + (pos + 1) * HEAD_DIM]
            k = qkv_ref[:, base + 256 + kv * HEAD_DIM:base + 256 + (kv + 1) * HEAD_DIM]
            v = qkv_ref[:, base + 384 + kv * HEAD_DIM:base + 384 + (kv + 1) * HEAD_DIM]
            s = _qk(q, k)
            m = jnp.max(s, axis=-1, keepdims=True)
            if grp == 0:
                m = jnp.maximum(m, sink_ref[h])
            p = jnp.exp(s - m)
            l = jnp.sum(p, axis=-1, keepdims=True)
            if grp == 0:
                l = l + jnp.exp(sink_ref[h] - m)
            o = jnp.dot(p.astype(BF16), v, preferred_element_type=F32) / l
            o_ref[:, grp * 256 + h * HEAD_DIM:grp * 256 + (h + 1) * HEAD_DIM] = o.astype(BF16)


def _attn_ctx(sink, qkv, *, batch, seq):
    return pl.pallas_call(
        _attn_ctx_kernel,
        grid=(batch,),
        in_specs=[
            pl.BlockSpec(memory_space=pltpu.SMEM),
            pl.BlockSpec((seq, 1024), lambda b: (b, 0)),
        ],
        out_specs=pl.BlockSpec((seq, 512), lambda b: (b, 0)),
        out_shape=jax.ShapeDtypeStruct((batch * seq, 512), BF16),
        compiler_params=_cparams(("parallel",)),
        name="attn_ctx",
    )(sink, qkv)


def _attn_lat_kernel(sink_ref, q_ref, k_ref, vt_ref, kcs_ref, vcs_ref, kca_ref, vca_ref, o_ref,
                     qm_scr, m_scr, acc_scr, s_scr, *, tq, seq, kc):
    i = pl.program_id(1)
    q0 = i * tq
    win = tq + 2 * WINDOW
    ws = pl.multiple_of(jnp.clip(q0 - WINDOW, 0, seq - win), WINDOW)
    kpos = ws + lax.broadcasted_iota(jnp.int32, (win, Q_PER_KV * tq), 0)
    qpos = q0 + (lax.broadcasted_iota(jnp.int32, (win, Q_PER_KV * tq), 1) & (tq - 1))
    in_band = jnp.abs(qpos - kpos) <= WINDOW
    lane = lax.broadcasted_iota(jnp.int32, (Q_PER_KV * tq, KV_W), 1)
    n_chunks = seq // kc
    n_kv = N_Q_HEADS // Q_PER_KV

    def update(kv, s, vt):
        m_old = m_scr[kv]
        m_new = jnp.maximum(m_old, jnp.max(s, axis=0, keepdims=True))
        p = jnp.exp2(s - m_new).astype(BF16)
        acc_scr[kv] = (jnp.exp2(m_old - m_new) * acc_scr[kv]
                       + jnp.dot(vt, p, preferred_element_type=F32))
        m_scr[kv] = m_new

    def sink_row(kv):
        return jnp.concatenate(
            [jnp.full((1, tq), sink_ref[Q_PER_KV * kv + j] * LOG2E, F32) for j in range(Q_PER_KV)], axis=1)

    for grp in range(2):
        acc_scr[...] = jnp.zeros(acc_scr.shape, F32)
        base = grp * 512
        q_plain = jnp.concatenate([q_ref[:, base:base + KV_W], q_ref[:, base + KV_W:base + 2 * KV_W]], axis=0)
        q_rope = jnp.concatenate([q_ref[:, base + 256:base + 256 + KV_W],
                                  q_ref[:, base + 256 + KV_W:base + 256 + 2 * KV_W]], axis=0)
        k_ctx = kcs_ref[...] if grp == 0 else kca_ref[...]
        vc_ref = vcs_ref if grp == 0 else vca_ref
        s_ctx = []
        for kv in range(n_kv):
            keep = (lane < HEAD_DIM) if kv == 0 else (lane >= HEAD_DIM)
            qm_scr[kv] = jnp.where(keep, q_rope, 0)
            s_ctx.append(_qk(k_ctx, jnp.where(keep, q_plain, 0)))
            m_scr[kv] = sink_row(kv) if grp == 0 else jnp.full((1, Q_PER_KV * tq), NEG_BIG, F32)
        if grp == 0:
            k_win = k_ref[pl.ds(ws, win), 0:KV_W]
            s_win = [jnp.where(in_band, _qk(k_win, qm_scr[kv]), NEG_BIG) for kv in range(n_kv)]
            for kv in range(n_kv):
                update(kv, s_ctx[kv], vc_ref[kv * 128:(kv + 1) * 128, :])
            for kv in range(n_kv):
                update(kv, s_win[kv], vt_ref[kv * 128:(kv + 1) * 128, pl.ds(ws, win)])
        else:
            def scores_to(slot, c):
                k_c = k_ref[pl.ds(pl.multiple_of(c * kc, kc), kc), KV_W:2 * KV_W]
                for kv in range(n_kv):
                    s_scr[slot, kv] = _qk(k_c, qm_scr[kv])

            def update_from(slot, c):
                cols = pl.ds(pl.multiple_of(c * kc, kc), kc)
                for kv in range(n_kv):
                    update(kv, s_scr[slot, kv], vt_ref[256 + kv * 128:256 + (kv + 1) * 128, cols])

            def chunk_pair(j, carry):
                c = 2 * j
                scores_to(1, c + 1)
                update_from(0, c)
                scores_to(0, c + 2)
                update_from(1, c + 1)
                return carry

            scores_to(0, 0)
            for kv in range(n_kv):
                update(kv, s_ctx[kv], vc_ref[kv * 128:(kv + 1) * 128, :])
            lax.fori_loop(0, n_chunks // 2 - 1, chunk_pair, 0)
            scores_to(1, n_chunks - 1)
            update_from(0, n_chunks - 2)
            update_from(1, n_chunks - 1)
        outs = []
        for kv in range(n_kv):
            acc = acc_scr[kv]
            den = acc[HEAD_DIM:HEAD_DIM + 1, :]
            if grp == 0:
                den = den + jnp.exp2(sink_row(kv) - m_scr[kv])
            o = acc[0:HEAD_DIM, :] / den
            outs += [o[:, j * tq:(j + 1) * tq] for j in range(Q_PER_KV)]
        o_t = jnp.concatenate(outs, axis=0)
        o_ref[:, grp * 256:(grp + 1) * 256] = jnp.transpose(o_t).astype(BF16)


def _attn_lat(sink, qkv, vt, kc_s, vc_s, kc_a, vc_a, *, batch, seq, tq, kc=512):
    nq = seq // tq
    past = kc_s.shape[1]
    per_b3 = lambda b, i: (b, 0, 0)
    return pl.pallas_call(
        functools.partial(_attn_lat_kernel, tq=tq, seq=seq, kc=kc),
        grid=(batch, nq),
        in_specs=[
            pl.BlockSpec(memory_space=pltpu.SMEM),
            pl.BlockSpec((tq, 1024), lambda b, i: (b * nq + i, 0)),
            pl.BlockSpec((seq, 256), lambda b, i: (b, 4)),
            pl.BlockSpec((None, 512, seq), per_b3),
            pl.BlockSpec((None, past, KV_W), per_b3),
            pl.BlockSpec((None, 256, past), per_b3),
            pl.BlockSpec((None, past, KV_W), per_b3),
            pl.BlockSpec((None, 256, past), per_b3),
        ],
        out_specs=pl.BlockSpec((tq, 512), lambda b, i: (b * nq + i, 0)),
        out_shape=jax.ShapeDtypeStruct((batch * seq, 512), BF16),
        scratch_shapes=[
            pltpu.VMEM((N_Q_HEADS // Q_PER_KV, Q_PER_KV * tq, KV_W), BF16),
            pltpu.VMEM((N_Q_HEADS // Q_PER_KV, 1, Q_PER_KV * tq), F32),
            pltpu.VMEM((N_Q_HEADS // Q_PER_KV, 128, Q_PER_KV * tq), F32),
            pltpu.VMEM((2, N_Q_HEADS // Q_PER_KV, kc, Q_PER_KV * tq), F32),
        ],
        compiler_params=_cparams(("parallel", "parallel")),
        name="attn_lat",
    )(sink, qkv, qkv, vt, kc_s, vc_s, kc_a, vc_a)


def _ssm_kernel(u_ref, bm_ref, cm_ref, a_ref, s0_ref, y_ref, fin_ref, bu_scr, y_scr, st_scr,
                *, lc, n_sub, reverse):
    n = pl.program_id(1)

    @pl.when(n == 0)
    def _():
        st_scr[...] = s0_ref[...]

    steps = lc // n_sub
    rows = steps * SUBLANES
    ar = jnp.broadcast_to(a_ref[0:1, :], (SUBLANES, N_STATE))
    ai = jnp.broadcast_to(a_ref[1:2, :], (SUBLANES, N_STATE))

    def project_in(i):
        u = u_ref[i * steps:(i + 1) * steps].reshape(rows, W_GRP).astype(BF16)
        bu_scr[i * rows:(i + 1) * rows, :] = jnp.dot(u, bm_ref[...], preferred_element_type=F32)

    def recurrence(i, carry):
        sr, si = carry
        ts = range(steps - 1, -1, -1) if reverse else range(steps)
        for t in ts:
            r = slice(i * rows + t * SUBLANES, i * rows + (t + 1) * SUBLANES)
            nr = ar * sr - ai * si + bu_scr[r, 0:N_STATE]
            ni = ar * si + ai * sr + bu_scr[r, N_STATE:2 * N_STATE]
            bu_scr[r, 0:N_STATE] = nr
            bu_scr[r, N_STATE:2 * N_STATE] = ni
            sr, si = nr, ni
        return sr, si

    def project_out(i):
        s = bu_scr[i * rows:(i + 1) * rows, :].astype(BF16)
        y = jnp.dot(s, cm_ref[...], preferred_element_type=F32)
        for half in range(W_GRP // LANES):
            y_scr[half, i * rows:(i + 1) * rows, :] = y[:, half * LANES:(half + 1) * LANES]

    order = list(range(n_sub - 1, -1, -1) if reverse else range(n_sub))
    carry = (st_scr[:, 0:N_STATE], st_scr[:, N_STATE:2 * N_STATE])
    project_in(order[0])
    for j, i in enumerate(order):
        if j + 1 < n_sub:
            project_in(order[j + 1])
        carry = recurrence(i, carry)
        project_out(i)
    st_scr[:, 0:N_STATE] = carry[0]
    st_scr[:, N_STATE:2 * N_STATE] = carry[1]
    for b in range(SUBLANES):
        for half in range(W_GRP // LANES):
            c0 = b * W_GRP + half * LANES
            y_ref[:, c0:c0 + LANES] = y_scr[half, pl.ds(b, lc, stride=SUBLANES), :]

    @pl.when(n == pl.num_programs(1) - 1)
    def _():
        fin_ref[...] = st_scr[...]


def _ssm(u_t, bmat, cmat, a, s0, *, batch, seq, lc, reverse):
    nch = seq // lc
    chunk = (lambda n: nch - 1 - n) if reverse else (lambda n: n)
    const = lambda b, n: (0, 0)
    return pl.pallas_call(
        functools.partial(_ssm_kernel, lc=lc, n_sub=4, reverse=reverse),
        grid=(batch // SUBLANES, nch),
        in_specs=[
            pl.BlockSpec((lc, SUBLANES, W_GRP), lambda b, n: (chunk(n), b, 0)),
            pl.BlockSpec((W_GRP, 2 * N_STATE), const),
            pl.BlockSpec((2 * N_STATE, W_GRP), const),
            pl.BlockSpec((2, N_STATE), const),
            pl.BlockSpec((SUBLANES, 2 * N_STATE), lambda b, n: (b, 0)),
        ],
        out_specs=[
            pl.BlockSpec((lc, SUBLANES * W_GRP), lambda b, n: (chunk(n), b)),
            pl.BlockSpec((SUBLANES, 2 * N_STATE), lambda b, n: (b, 0)),
        ],
        out_shape=[
            jax.ShapeDtypeStruct((seq, batch * W_GRP), F32),
            jax.ShapeDtypeStruct((batch, 2 * N_STATE), F32),
        ],
        scratch_shapes=[
            pltpu.VMEM((lc * SUBLANES, 2 * N_STATE), F32),
            pltpu.VMEM((W_GRP // LANES, lc * SUBLANES, LANES), F32),
            pltpu.VMEM((SUBLANES, 2 * N_STATE), F32),
        ],
        compiler_params=_cparams(("parallel", "arbitrary")),
        name="ssm_scan",
    )(u_t, bmat, cmat, a, s0)


def _dft_kernel(t1c_ref, t1s_ref, t2c_ref, t2s_ref, pc_ref, ps_ref, o_ref, lc_scr, ls_scr, acc_scr,
                *, tm, scale):
    kb = pl.program_id(2)
    t2c = t2c_ref[...]
    t2s = t2s_ref[...]
    for a in range(tm // DFT_RADIX):
        c1 = t1c_ref[a:a + 1, :]
        s1 = t1s_ref[a:a + 1, :]
        rows = slice(a * DFT_RADIX, (a + 1) * DFT_RADIX)
        lc_scr[rows, :] = (c1 * t2c - s1 * t2s).astype(BF16)
        ls_scr[rows, :] = (s1 * t2c + c1 * t2s).astype(BF16)
    part = (jnp.dot(lc_scr[...], pc_ref[...], preferred_element_type=F32)
            - jnp.dot(ls_scr[...], ps_ref[...], preferred_element_type=F32))

    @pl.when(kb == 0)
    def _():
        acc_scr[...] = part

    @pl.when(kb > 0)
    def _():
        acc_scr[...] += part

    @pl.when(kb == pl.num_programs(2) - 1)
    def _():
        o_ref[...] = (acc_scr[...] * scale).astype(BF16)


def _dft(tabs, pc, ps, *, seq, tm, tk, tn):
    nc = pc.shape[1]
    t1c, t1s, t2c, t2s = tabs
    ra = tm // DFT_RADIX
    scale = 1.0 / math.sqrt(seq * W_GRP)
    return pl.pallas_call(
        functools.partial(_dft_kernel, tm=tm, scale=scale),
        grid=(seq // tm, nc // tn, seq // tk),
        in_specs=[
            pl.BlockSpec((ra, tk), lambda i, j, k: (i, k)),
            pl.BlockSpec((ra, tk), lambda i, j, k: (i, k)),
            pl.BlockSpec((DFT_RADIX, tk), lambda i, j, k: (0, k)),
            pl.BlockSpec((DFT_RADIX, tk), lambda i, j, k: (0, k)),
            pl.BlockSpec((tk, tn), lambda i, j, k: (k, j)),
            pl.BlockSpec((tk, tn), lambda i, j, k: (k, j)),
        ],
        out_specs=pl.BlockSpec((tm, tn), lambda i, j, k: (i, j)),
        out_shape=jax.ShapeDtypeStruct((seq, nc), BF16),
        scratch_shapes=[pltpu.VMEM((tm, tk), BF16), pltpu.VMEM((tm, tk), BF16), pltpu.VMEM((tm, tn), F32)],
        compiler_params=_cparams(("parallel", "parallel", "arbitrary")),
        name="pos_dft",
    )(t1c, t1s, t2c, t2s, pc, ps)


def _mix_out_kernel(h_ref, mod_ref, yf_ref, yb_ref, u_ref, att_ref, f_ref, d_ref, wglu_ref, bglu_ref,
                    wf_ref, bf_ref, wo_ref, o_ref, m_scr):
    y = d_ref[...] * u_ref[...] + yf_ref[...] + yb_ref[...]
    y = y * (0.5 * (1.0 + jnp.tanh(math.sqrt(2.0 / math.pi) * (y + 0.044715 * (y * y * y)))))
    z = jnp.dot(y.astype(BF16), wglu_ref[...], preferred_element_type=F32) + bglu_ref[...]
    m_scr[:, 0:256] = (y * jax.nn.sigmoid(z)).astype(BF16)
    m_scr[:, 256:768] = att_ref[...]
    m_scr[:, 768:1024] = (jnp.dot(f_ref[...], wf_ref[...], preferred_element_type=F32) + bf_ref[...]).astype(BF16)
    mixed = jnp.dot(m_scr[...], wo_ref[...], preferred_element_type=F32)
    o_ref[...] = h_ref[...] + mod_ref[0, 5:6, :] * mixed


def _mix_out(h, mod, y_fwd, y_bwd, u_t, att, f, d, wglu, bglu, wf, bfn, wo, *, batch, seq, tm, cond_base,
             per_batch_cond):
    t = batch * seq
    tiles = seq // tm
    cond = lambda i: (cond_base + (i // tiles if per_batch_cond else 0), 0, 0)
    tcol = lambda i: (i % tiles, i // tiles)
    const = lambda i: (0, 0)
    return pl.pallas_call(
        _mix_out_kernel,
        grid=(t // tm,),
        in_specs=[
            pl.BlockSpec((tm, D_MODEL), lambda i: (i, 0)),
            pl.BlockSpec((1, N_MOD, D_MODEL), cond),
            pl.BlockSpec((tm, W_GRP), tcol),
            pl.BlockSpec((tm, W_GRP), tcol),
            pl.BlockSpec((tm, W_GRP), tcol),
            pl.BlockSpec((tm, 512), lambda i: (i, 0)),
            pl.BlockSpec((tm, W_GRP), tcol),
            pl.BlockSpec((1, W_GRP), const),
            pl.BlockSpec((W_GRP, W_GRP), const),
            pl.BlockSpec((1, W_GRP), const),
            pl.BlockSpec((W_GRP, W_GRP), const),
            pl.BlockSpec((1, W_GRP), const),
            pl.BlockSpec((D_MODEL, D_MODEL), const),
        ],
        out_specs=pl.BlockSpec((tm, D_MODEL), lambda i: (i, 0)),
        out_shape=jax.ShapeDtypeStruct((t, D_MODEL), F32),
        scratch_shapes=[pltpu.VMEM((tm, D_MODEL), BF16)],
        compiler_params=_cparams(("parallel",)),
        name="mix_out",
    )(h, mod, y_fwd, y_bwd, u_t, att, f, d, wglu, bglu, wf, bfn, wo)


def _rope_tables(seq):
    rows = seq // GRID_W
    row_id = jnp.repeat(jnp.arange(rows), GRID_W).astype(F32)
    col_id = jnp.tile(jnp.arange(GRID_W), rows).astype(F32)
    n_freq = HEAD_DIM // 4
    inv = ROPE_BASE ** (-jnp.arange(n_freq, dtype=F32) / n_freq)
    ang = jnp.concatenate([row_id[:, None] * inv, col_id[:, None] * inv], axis=-1)
    cos, sin = jnp.cos(ang), jnp.sin(ang)
    cos_full = jnp.tile(jnp.concatenate([cos, cos], axis=-1), (1, N_Q_HEADS))
    sin_signed = jnp.tile(jnp.concatenate([-sin, sin], axis=-1), (1, N_Q_HEADS))
    return cos_full, sin_signed


def _angle_table(mult, n, period):
    m = (mult[:, None] * n[None, :]) % period
    th = m.astype(F32) * (2.0 * math.pi / period)
    return jnp.cos(th), jnp.sin(th)


def _dft_tables(seq):
    n = jnp.arange(seq, dtype=jnp.int32)
    t1c, t1s = _angle_table(DFT_RADIX * jnp.arange(seq // DFT_RADIX, dtype=jnp.int32), n, seq)
    t2c, t2s = _angle_table(jnp.arange(DFT_RADIX, dtype=jnp.int32), n, seq)
    return t1c, t1s, t2c, t2s


def _ssm_params(lam_re, lam_im, b_re, b_im, c_re, c_im, log_dt):
    dt = jnp.exp(log_dt)[..., None]
    mag = jnp.exp(lam_re * dt)
    ar = mag * jnp.cos(lam_im * dt)
    ai = mag * jnp.sin(lam_im * dt)
    den = lam_re * lam_re + lam_im * lam_im
    qr = ((ar - 1.0) * lam_re + ai * lam_im) / den
    qi = (ai * lam_re - (ar - 1.0) * lam_im) / den
    bb_re = qr[..., None] * b_re - qi[..., None] * b_im
    bb_im = qr[..., None] * b_im + qi[..., None] * b_re
    eye = jnp.eye(N_SSM_GROUPS, dtype=F32)

    def block_diag(x):
        y = jnp.transpose(x, (0, 1, 3, 2))[:, :, :, None, :] * eye[None, :, None, :, None]
        return y.reshape(2, x.shape[1] * x.shape[3], x.shape[1] * x.shape[2])

    bmat = jnp.concatenate([block_diag(bb_re), block_diag(bb_im)], axis=-1).astype(BF16)
    cmat = jnp.concatenate([block_diag(c_re), block_diag(-c_im)], axis=1).astype(BF16)
    a2 = jnp.stack([ar.reshape(2, N_STATE), ai.reshape(2, N_STATE)], axis=1)
    return a2, bmat, cmat


def _permute_q_heads(w_in):
    cols = jnp.arange(P_IN, dtype=jnp.int32)
    for base in (256, 768):
        blk = jnp.concatenate([base + h * HEAD_DIM + jnp.arange(HEAD_DIM, dtype=jnp.int32) for h in Q_HEAD_ORDER])
        cols = cols.at[base:base + N_Q_HEADS * HEAD_DIM].set(blk)
    return jnp.take(w_in, cols, axis=1)


def _cache_layout(cache):
    b, _, past, nkv, dh = cache.shape
    keys = cache[:, 0].reshape(b, past, nkv * dh).astype(BF16)
    vt = jnp.transpose(cache[:, 1], (0, 2, 3, 1))
    vt = jnp.concatenate([vt, jnp.ones_like(vt)], axis=2).reshape(b, 2 * nkv * dh, past).astype(BF16)
    return keys, vt


def kernel(x_prompt, x_sample, cache_swa_kv, cache_axial_kv, state_ssm, c, c_ctx, w_mod, b_mod, norm_ffn1, norm_mix, norm_ffn2, ffn1_w_gate, ffn1_w_up, ffn1_w_down, ffn2_w_gate, ffn2_w_up, ffn2_w_down, w_in, w_out, ssm_lambda_re, ssm_lambda_im, ssm_b_re, ssm_b_im, ssm_c_re, ssm_c_im, ssm_log_dt, ssm_d, ssm_w_glu, ssm_b_glu, swa_sink, ax_q_norm, ax_k_norm, fnet_w, fnet_b, final_norm):
    depth = w_mod.shape[0]
    cb, cl, _ = x_prompt.shape
    lb, ll, _ = x_sample.shape

    cond = jnp.zeros((16, D_MODEL), F32).at[0].set(c_ctx).at[1:1 + lb].set(c)
    mod_all = _modulation(cond, w_mod, b_mod.reshape(depth, 1, N_MOD * D_MODEL))
    mod_all = mod_all.reshape(depth, 16, N_MOD, D_MODEL)

    e_heads = jnp.kron(jnp.eye(256 // HEAD_DIM, dtype=F32), jnp.ones((HEAD_DIM, HEAD_DIM), F32)) / HEAD_DIM
    e_heads = e_heads.astype(BF16)
    kc = jnp.arange(W_GRP, dtype=jnp.int32)
    cc, sc = _angle_table(kc, kc, W_GRP)
    cs_chan = jnp.concatenate([cc, sc], axis=-1).astype(BF16)
    rope_tabs = _rope_tables(ll)
    dft_ctx = _dft_tables(cl)
    dft_lat = _dft_tables(ll)
    fg = final_norm.reshape(1, D_MODEL)

    h_ctx = x_prompt.reshape(cb * cl, D_MODEL)
    h_lat = x_sample.reshape(lb * ll, D_MODEL)
    tm_lat = 512
    swa_list, ax_list, ssm_list = [], [], []
    for l in range(depth):
        mod = mod_all[l]
        bf = lambda w: w[l].astype(BF16)
        row = lambda v: v[l].reshape(1, -1)
        ffn1 = (row(norm_ffn1), bf(ffn1_w_gate), bf(ffn1_w_up), bf(ffn1_w_down), fg)
        ffn2 = (row(norm_ffn2), bf(ffn2_w_gate), bf(ffn2_w_up), bf(ffn2_w_down), fg)
        w_in_l, w_out_l = _permute_q_heads(w_in[l]).astype(BF16), bf(w_out)
        qn = jnp.tile(ax_q_norm[l], N_Q_HEADS).reshape(1, 256)
        kn = jnp.tile(ax_k_norm[l], KV_W // HEAD_DIM).reshape(1, KV_W)
        a2, bmat, cmat = _ssm_params(ssm_lambda_re[l], ssm_lambda_im[l], ssm_b_re[l], ssm_b_im[l],
                                     ssm_c_re[l], ssm_c_im[l], ssm_log_dt[l])
        mix_out_w = (row(ssm_d), bf(ssm_w_glu), row(ssm_b_glu), bf(fnet_w), row(fnet_b), w_out_l)
        sink = swa_sink[l]
        last = l == depth - 1

        h_ctx = _ffn(h_ctx, mod, *ffn1, mod_base=0, tm=512, tiles_per_cond=1 << 30, cond_base=0, final_norm=False)
        u_t, qkv, pc, ps, kv_s, kv_a = _mix_in(h_ctx, mod, row(norm_mix), w_in_l, qn, kn, e_heads, cs_chan, None,
                                               batch=cb, seq=cl, tm=cl, cond_base=0, rope=False)
        u3 = u_t.reshape(cl, cb, W_GRP)
        zero_state = jnp.zeros((cb, 2 * N_STATE), F32)
        y_dirs, s_fins = zip(*[_ssm(u3, bmat[d], cmat[d], a2[d], zero_state, batch=cb, seq=cl, lc=128,
                                    reverse=bool(d)) for d in range(2)])
        att = _attn_ctx(sink, qkv, batch=cb, seq=cl)
        f = _dft(dft_ctx, pc, ps, seq=cl, tm=cl, tk=cl, tn=2048)
        h_ctx = _mix_out(h_ctx, mod, *y_dirs, u_t, att, f, *mix_out_w,
                         batch=cb, seq=cl, tm=cl, cond_base=0, per_batch_cond=False)
        h_ctx = _ffn(h_ctx, mod, *ffn2, mod_base=6, tm=512, tiles_per_cond=1 << 30, cond_base=0, final_norm=last)
        swa_list.append(kv_s.reshape(cb, 2, cl, KV_W // HEAD_DIM, HEAD_DIM))
        ax_list.append(kv_a.reshape(cb, 2, cl, KV_W // HEAD_DIM, HEAD_DIM))
        s_fin = jnp.stack(s_fins, axis=0).reshape(2, cb, 2, N_SSM_GROUPS, SSM_STATE)
        ssm_list.append(jnp.transpose(s_fin, (1, 0, 3, 4, 2)))

        tpc = ll // tm_lat
        h_lat = _ffn(h_lat, mod, *ffn1, mod_base=0, tm=tm_lat, tiles_per_cond=tpc, cond_base=1, final_norm=False)
        u_t, qkv, pc, ps, vt = _mix_in(h_lat, mod, row(norm_mix), w_in_l, qn, kn, e_heads, cs_chan, rope_tabs,
                                       batch=lb, seq=ll, tm=tm_lat, cond_base=1, rope=True)
        s0 = jnp.transpose(state_ssm[:, l], (1, 0, 4, 2, 3)).reshape(2, lb, 2 * N_STATE)
        u3 = u_t.reshape(ll, lb, W_GRP)
        y_dirs = [_ssm(u3, bmat[d], cmat[d], a2[d], s0[d], batch=lb, seq=ll, lc=128, reverse=bool(d))[0]
                  for d in range(2)]
        kc_s, vc_s = _cache_layout(cache_swa_kv[:, l])
        kc_a, vc_a = _cache_layout(cache_axial_kv[:, l])
        att = _attn_lat(sink, qkv, vt, kc_s, vc_s, kc_a, vc_a, batch=lb, seq=ll, tq=256)
        f = _dft(dft_lat, pc, ps, seq=ll, tm=512, tk=1024, tn=2048)
        h_lat = _mix_out(h_lat, mod, *y_dirs, u_t, att, f, *mix_out_w,
                         batch=lb, seq=ll, tm=tm_lat, cond_base=1, per_batch_cond=True)
        h_lat = _ffn(h_lat, mod, *ffn2, mod_base=6, tm=tm_lat, tiles_per_cond=tpc, cond_base=1, final_norm=last)

    y_prompt = h_ctx.reshape(cb, cl, D_MODEL)
    y_sample = h_lat.reshape(lb, ll, D_MODEL)
    return (y_prompt, y_sample, jnp.stack(swa_list, axis=1), jnp.stack(ax_list, axis=1),
            jnp.stack(ssm_list, axis=1))
```

```python
import functools
import math

import jax
import jax.numpy as jnp
from jax import lax
from jax.experimental import pallas as pl
from jax.experimental.pallas import tpu as pltpu

F32 = jnp.float32
BF16 = jnp.bfloat16

D_MODEL = 1024
D_FF = 2816
N_MOD = 9
EPS = 1e-6
HEAD_DIM = 64
HALF_HEAD = HEAD_DIM // 2
N_Q_HEADS = 4
Q_PER_KV = 2
W_GRP = 256
KV_W = 128
P_IN = 1536
N_SSM_GROUPS = 16
SSM_GROUP = 16
SSM_STATE = 64
N_STATE = N_SSM_GROUPS * SSM_STATE
WINDOW = 128
GRID_W = 64
ROPE_BASE = 10000.0
NEG_BIG = -1e30
LOG2E = math.log2(math.e)
Q_HEAD_ORDER = (0, 2, 1, 3)
N_KV_HEADS = N_Q_HEADS // Q_PER_KV
VT_ROWS = 80

FF_CHUNKS = (1024, 1024, 768)
SUBLANES = 8
LANES = 128
DFT_RADIX = 64
VMEM_LIMIT = 56 * 1024 * 1024


def _cparams(sem):
    return pltpu.CompilerParams(dimension_semantics=sem, vmem_limit_bytes=VMEM_LIMIT)


def _adaln(h, g, sc, sh):
    ms = jnp.mean(h * h, axis=-1, keepdims=True)
    return (h * lax.rsqrt(ms + EPS) * g) * (1.0 + sc) + sh


def _mod_kernel(c_ref, w_ref, b_ref, o_ref):
    c = c_ref[...]
    a = (c * jax.nn.sigmoid(c)).astype(BF16)
    o_ref[...] = jnp.dot(a, w_ref[...].astype(BF16), preferred_element_type=F32) + b_ref[...]


def _modulation(cond, w_mod, b_mod):
    depth = w_mod.shape[0]
    n = N_MOD * D_MODEL
    tn = 1024
    return pl.pallas_call(
        _mod_kernel,
        grid=(depth, n // tn),
        in_specs=[
            pl.BlockSpec((16, D_MODEL), lambda l, j: (0, 0)),
            pl.BlockSpec((None, D_MODEL, tn), lambda l, j: (l, 0, j)),
            pl.BlockSpec((None, 1, tn), lambda l, j: (l, 0, j)),
        ],
        out_specs=pl.BlockSpec((None, 16, tn), lambda l, j: (l, 0, j)),
        out_shape=jax.ShapeDtypeStruct((depth, 16, n), F32),
        compiler_params=_cparams(("parallel", "parallel")),
        name="modulation",
    )(cond, w_mod, b_mod)


def _ffn_kernel(h_ref, mod_ref, g_ref, wg_ref, wu_ref, wd_ref, fg_ref, o_ref, *, mod_base, final_norm):
    h = h_ref[...]
    x = _adaln(h, g_ref[...], mod_ref[0, mod_base + 1:mod_base + 2, :],
               mod_ref[0, mod_base:mod_base + 1, :]).astype(BF16)
    acc = None
    c0 = 0
    for width in FF_CHUNKS:
        gate = jnp.dot(x, wg_ref[:, c0:c0 + width], preferred_element_type=F32)
        up = jnp.dot(x, wu_ref[:, c0:c0 + width], preferred_element_type=F32)
        hg = 0.5 * gate
        a = (hg * (1.0 + jnp.tanh(hg)) * up).astype(BF16)
        part = jnp.dot(a, wd_ref[c0:c0 + width, :], preferred_element_type=F32)
        acc = part if acc is None else acc + part
        c0 += width
    hn = h + (0.5 * mod_ref[0, mod_base + 2:mod_base + 3, :]) * acc
    if final_norm:
        ms = jnp.mean(hn * hn, axis=-1, keepdims=True)
        hn = hn * lax.rsqrt(ms + EPS) * fg_ref[...]
    o_ref[...] = hn


def _ffn(h, mod, g, wg, wu, wd, fg, *, mod_base, tm, tiles_per_cond, cond_base, final_norm):
    t = h.shape[0]
    cond = lambda i: (cond_base + i // tiles_per_cond, 0, 0)
    const = lambda i: (0, 0)
    resident = dict(pipeline_mode=pl.Buffered(1))
    kern = functools.partial(_ffn_kernel, mod_base=mod_base, final_norm=final_norm)
    return pl.pallas_call(
        kern,
        grid=(t // tm,),
        in_specs=[
            pl.BlockSpec((tm, D_MODEL), lambda i: (i, 0)),
            pl.BlockSpec((1, N_MOD, D_MODEL), cond),
            pl.BlockSpec((1, D_MODEL), const),
            pl.BlockSpec((D_MODEL, D_FF), const, **resident),
            pl.BlockSpec((D_MODEL, D_FF), const, **resident),
            pl.BlockSpec((D_FF, D_MODEL), const, **resident),
            pl.BlockSpec((1, D_MODEL), const),
        ],
        out_specs=pl.BlockSpec((tm, D_MODEL), lambda i: (i, 0)),
        out_shape=jax.ShapeDtypeStruct((t, D_MODEL), F32),
        compiler_params=_cparams(("parallel",)),
        name="ffn",
    )(h, mod, g, wg, wu, wd, fg)


def _head_mean_sq(x, e):
    x2 = x * x
    hi = x2.astype(BF16)
    lo = (x2 - hi.astype(F32)).astype(BF16)
    return jnp.dot(hi, e, preferred_element_type=F32) + jnp.dot(lo, e, preferred_element_type=F32)


def _swap_halves(x):
    w = x.shape[-1]
    lane = lax.broadcasted_iota(jnp.int32, x.shape, 1)
    first = (lane & HALF_HEAD) == 0
    return jnp.where(first, pltpu.roll(x, w - HALF_HEAD, 1), pltpu.roll(x, HALF_HEAD, 1))


def _mix_in_kernel(*refs, rope):
    if rope:
        (h_ref, mod_ref, g_ref, w_ref, qn_ref, kn_ref, e_ref, cs_ref, cos_ref, sin_ref,
         u_ref, qkv_ref, pc_ref, ps_ref, vt_ref) = refs
    else:
        (h_ref, mod_ref, g_ref, w_ref, qn_ref, kn_ref, e_ref, cs_ref,
         u_ref, qkv_ref, pc_ref, ps_ref, kvs_ref, kva_ref) = refs
    x = _adaln(h_ref[...], g_ref[...], mod_ref[0, 4:5, :], mod_ref[0, 3:4, :]).astype(BF16)
    proj = jnp.dot(x, w_ref[...], preferred_element_type=F32)
    u_ssm = proj[:, 0:256]
    q_s = proj[:, 256:512]
    k_s = proj[:, 512:640]
    v_s = proj[:, 640:768]
    q_a = proj[:, 768:1024]
    k_a = proj[:, 1024:1152]
    v_a = proj[:, 1152:1280]
    u_f = proj[:, 1280:1536]

    e = e_ref[...]
    q_a = q_a * lax.rsqrt(_head_mean_sq(q_a, e) + EPS) * qn_ref[...]
    k_a = k_a * lax.rsqrt(_head_mean_sq(k_a, e[0:KV_W, 0:KV_W]) + EPS) * kn_ref[...]

    u_ref[...] = u_ssm
    p = jnp.dot(u_f.astype(BF16), cs_ref[...], preferred_element_type=F32)
    pc_ref[...] = p[:, 0:256].astype(BF16)
    ps_ref[...] = p[:, 256:512].astype(BF16)

    if rope:
        scale = LOG2E * HEAD_DIM ** -0.5
        cos = cos_ref[...]
        sin = sin_ref[...]
        rot = lambda t, w: t * cos[:, 0:w] + _swap_halves(t) * sin[:, 0:w]
        qkv_ref[:, 0:256] = (q_s * scale).astype(BF16)
        qkv_ref[:, 256:512] = (rot(q_s, 256) * scale).astype(BF16)
        qkv_ref[:, 512:768] = (q_a * scale).astype(BF16)
        qkv_ref[:, 768:1024] = (rot(q_a, 256) * scale).astype(BF16)
        qkv_ref[:, 1024:1152] = rot(k_s, KV_W).astype(BF16)
        qkv_ref[:, 1152:1280] = rot(k_a, KV_W).astype(BF16)
        ones = jnp.ones((VT_ROWS - HEAD_DIM, v_s.shape[0]), BF16)
        for grp, v in enumerate((v_s, v_a)):
            vt = jnp.transpose(v).astype(BF16)
            for kv in range(N_KV_HEADS):
                r0 = (grp * N_KV_HEADS + kv) * VT_ROWS
                vt_ref[r0:r0 + HEAD_DIM, :] = vt[kv * HEAD_DIM:(kv + 1) * HEAD_DIM, :]
                vt_ref[r0 + HEAD_DIM:r0 + VT_ROWS, :] = ones
    else:
        scale = HEAD_DIM ** -0.5
        qkv_ref[:, 0:256] = (q_s * scale).astype(BF16)
        qkv_ref[:, 256:384] = k_s.astype(BF16)
        qkv_ref[:, 384:512] = v_s.astype(BF16)
        qkv_ref[:, 512:768] = (q_a * scale).astype(BF16)
        qkv_ref[:, 768:896] = k_a.astype(BF16)
        qkv_ref[:, 896:1024] = v_a.astype(BF16)
        kvs_ref[0] = k_s
        kvs_ref[1] = v_s
        kva_ref[0] = k_a
        kva_ref[1] = v_a


def _mix_in(h, mod, g, w_in, qn, kn, e, cs, rope_tabs, *, batch, seq, tm, cond_base, rope):
    t = batch * seq
    tiles = seq // tm
    nc = batch * W_GRP
    cond = lambda i: (cond_base + (i // tiles if rope else 0), 0, 0)
    tcol = lambda i: (i % tiles, i // tiles)
    const = lambda i: (0, 0)
    in_specs = [
        pl.BlockSpec((tm, D_MODEL), lambda i: (i, 0)),
        pl.BlockSpec((1, N_MOD, D_MODEL), cond),
        pl.BlockSpec((1, D_MODEL), const),
        pl.BlockSpec((D_MODEL, P_IN), const),
        pl.BlockSpec((1, 256), const),
        pl.BlockSpec((1, KV_W), const),
        pl.BlockSpec((256, 256), const),
        pl.BlockSpec((256, 512), const),
    ]
    args = [h, mod, g, w_in, qn, kn, e, cs]
    qkv_w = 1280 if rope else 1024
    out_specs = [
        pl.BlockSpec((tm, W_GRP), tcol),
        pl.BlockSpec((tm, qkv_w), lambda i: (i, 0)),
        pl.BlockSpec((tm, W_GRP), tcol),
        pl.BlockSpec((tm, W_GRP), tcol),
    ]
    out_shape = [
        jax.ShapeDtypeStruct((seq, nc), F32),
        jax.ShapeDtypeStruct((t, qkv_w), BF16),
        jax.ShapeDtypeStruct((seq, nc), BF16),
        jax.ShapeDtypeStruct((seq, nc), BF16),
    ]
    if rope:
        in_specs += [pl.BlockSpec((tm, 256), lambda i: (i % tiles, 0))] * 2
        args += list(rope_tabs)
        out_specs.append(pl.BlockSpec((None, 2 * N_KV_HEADS * VT_ROWS, tm), lambda i: (i // tiles, 0, i % tiles)))
        out_shape.append(jax.ShapeDtypeStruct((batch, 2 * N_KV_HEADS * VT_ROWS, seq), BF16))
    else:
        assert tm == seq
        out_specs += [pl.BlockSpec((None, 2, seq, KV_W), lambda i: (i, 0, 0, 0))] * 2
        out_shape += [jax.ShapeDtypeStruct((batch, 2, seq, KV_W), F32)] * 2
    return pl.pallas_call(
        functools.partial(_mix_in_kernel, rope=rope),
        grid=(t // tm,),
        in_specs=in_specs,
        out_specs=out_specs,
        out_shape=out_shape,
        compiler_params=_cparams(("parallel",)),
        name="mix_in",
    )(*args)


def _qk(a, b):
    return lax.dot_general(a, b, (((1,), (1,)), ((), ())), preferred_element_type=F32)


def _attn_ctx_kernel(sink_ref, qkv_ref, o_ref):
    for grp in range(2):
        base = grp * 512
        for pos, h in enumerate(Q_HEAD_ORDER):
            kv = h // Q_PER_KV
            q = qkv_ref[:, base + pos * HEAD_DIM:base + (pos + 1) * HEAD_DIM]
            k = qkv_ref[:, base + 256 + kv * HEAD_DIM:base + 256 + (kv + 1) * HEAD_DIM]
            v = qkv_ref[:, base + 384 + kv * HEAD_DIM:base + 384 + (kv + 1) * HEAD_DIM]
            s = _qk(q, k)
            m = jnp.max(s, axis=-1, keepdims=True)
            if grp == 0:
                m = jnp.maximum(m, sink_ref[h])
            p = jnp.exp(s - m)
            l = jnp.sum(p, axis=-1, keepdims=True)
            if grp == 0:
                l = l + jnp.exp(sink_ref[h] - m)
            o = jnp.dot(p.astype(BF16), v, preferred_element_type=F32) / l
            o_ref[:, grp * 256 + h * HEAD_DIM:grp * 256 + (h + 1) * HEAD_DIM] = o.astype(BF16)


def _attn_ctx(sink, qkv, *, batch, seq):
    return pl.pallas_call(
        _attn_ctx_kernel,
        grid=(batch,),
        in_specs=[
            pl.BlockSpec(memory_space=pltpu.SMEM),
            pl.BlockSpec((seq, 1024), lambda b: (b, 0)),
        ],
        out_specs=pl.BlockSpec((seq, 512), lambda b: (b, 0)),
        out_shape=jax.ShapeDtypeStruct((batch * seq, 512), BF16),
        compiler_params=_cparams(("parallel",)),
        name="attn_ctx",
    )(sink, qkv)


def _attn_lat_kernel(sink_ref, zero_ref, q_ref, k_ref, vt_ref, kcs_ref, vcs_ref, kca_ref, vca_ref, o_ref,
                     qm_scr, m_scr, acc_scr, s0_scr, s1_scr, *, tq, seq, kc):
    i = pl.program_id(1)
    q0 = i * tq
    win = tq + 2 * WINDOW
    past = kcs_ref.shape[0]
    assert win <= kc and past <= kc
    ws = pl.multiple_of(jnp.clip(q0 - WINDOW, 0, seq - win), WINDOW)
    kpos = ws + lax.broadcasted_iota(jnp.int32, (win, Q_PER_KV * tq), 0)
    qpos = q0 + (lax.broadcasted_iota(jnp.int32, (win, Q_PER_KV * tq), 1) & (tq - 1))
    in_band = jnp.abs(qpos - kpos) <= WINDOW
    lane = lax.broadcasted_iota(jnp.int32, (Q_PER_KV * tq, KV_W), 1)

    def sink_row(kv):
        return jnp.concatenate(
            [jnp.full((1, tq), sink_ref[Q_PER_KV * kv + j] * LOG2E, F32) for j in range(Q_PER_KV)], axis=1)

    def vt_rows(grp, kv):
        r0 = (grp * N_KV_HEADS + kv) * VT_ROWS
        return slice(r0, r0 + VT_ROWS)

    acc_scr[...] = jnp.zeros(acc_scr.shape, F32)
    for grp in range(2):
        base = grp * 512
        q_plain = jnp.concatenate([q_ref[:, base:base + KV_W], q_ref[:, base + KV_W:base + 2 * KV_W]], axis=0)
        q_rope = jnp.concatenate([q_ref[:, base + 256:base + 256 + KV_W],
                                  q_ref[:, base + 256 + KV_W:base + 256 + 2 * KV_W]], axis=0)
        for kv in range(N_KV_HEADS):
            keep = (lane < HEAD_DIM) if kv == 0 else (lane >= HEAD_DIM)
            qm_scr[grp, 0, kv] = jnp.where(keep, q_plain, 0)
            qm_scr[grp, 1, kv] = jnp.where(keep, q_rope, 0)
            m_scr[grp, kv] = sink_row(kv) if grp == 0 else jnp.full((1, Q_PER_KV * tq), NEG_BIG, F32)

    jobs = [
        (0, past, lambda kv: _qk(kcs_ref[...], qm_scr[0, 0, kv]),
         lambda kv: vcs_ref[kv * VT_ROWS:(kv + 1) * VT_ROWS, :]),
        (0, win, lambda kv: jnp.where(in_band, _qk(k_ref[pl.ds(ws, win), 0:KV_W], qm_scr[0, 1, kv]), NEG_BIG),
         lambda kv: vt_ref[vt_rows(0, kv), pl.ds(ws, win)]),
        (1, past, lambda kv: _qk(kca_ref[...], qm_scr[1, 0, kv]),
         lambda kv: vca_ref[kv * VT_ROWS:(kv + 1) * VT_ROWS, :]),
    ]
    for c in range(seq // kc):
        jobs.append((1, kc, lambda kv, c=c: _qk(k_ref[c * kc:(c + 1) * kc, KV_W:2 * KV_W], qm_scr[1, 1, kv]),
                     lambda kv, c=c: vt_ref[vt_rows(1, kv), c * kc:(c + 1) * kc]))

    s_slots = (s0_scr, s1_scr)
    row0 = pl.multiple_of(zero_ref[0], SUBLANES)

    def scores_to(slot, job):
        _, n, scores, _ = job
        for kv in range(N_KV_HEADS):
            s_slots[slot][kv, 0:n, :] = scores(kv)

    def update_from(slot, job):
        grp, n, _, values = job
        for kv in range(N_KV_HEADS):
            s = s_slots[slot][kv, pl.ds(row0, n), :]
            m_old = m_scr[grp, kv]
            m_new = jnp.maximum(m_old, jnp.max(s, axis=0, keepdims=True))
            p = jnp.exp2(s - m_new).astype(BF16)
            acc_scr[grp, kv] = (jnp.exp2(m_old - m_new) * acc_scr[grp, kv]
                                + jnp.dot(values(kv), p, preferred_element_type=F32))
            m_scr[grp, kv] = m_new

    def finalize(grp):
        outs = []
        for kv in range(N_KV_HEADS):
            acc = acc_scr[grp, kv]
            den = acc[HEAD_DIM:HEAD_DIM + 1, :]
            if grp == 0:
                den = den + jnp.exp2(sink_row(kv) - m_scr[grp, kv])
            o = acc[0:HEAD_DIM, :] / den
            outs += [o[:, j * tq:(j + 1) * tq] for j in range(Q_PER_KV)]
        o_t = jnp.concatenate(outs, axis=0)
        o_ref[:, grp * 256:(grp + 1) * 256] = jnp.transpose(o_t).astype(BF16)

    scores_to(0, jobs[0])
    for j, job in enumerate(jobs):
        if j + 1 < len(jobs):
            scores_to((j + 1) % 2, jobs[j + 1])
        update_from(j % 2, job)
        if j + 1 == len(jobs) or jobs[j + 1][0] != job[0]:
            finalize(job[0])


def _attn_lat(sink, qkv, vt, kc_s, vc_s, kc_a, vc_a, *, batch, seq, tq, kc=512):
    nq = seq // tq
    past = kc_s.shape[1]
    per_b3 = lambda b, i: (b, 0, 0)
    return pl.pallas_call(
        functools.partial(_attn_lat_kernel, tq=tq, seq=seq, kc=kc),
        grid=(batch, nq),
        in_specs=[
            pl.BlockSpec(memory_space=pltpu.SMEM),
            pl.BlockSpec(memory_space=pltpu.SMEM),
            pl.BlockSpec((tq, 1024), lambda b, i: (b * nq + i, 0)),
            pl.BlockSpec((seq, 256), lambda b, i: (b, 4)),
            pl.BlockSpec((None, 2 * N_KV_HEADS * VT_ROWS, seq), per_b3),
            pl.BlockSpec((None, past, KV_W), per_b3),
            pl.BlockSpec((None, N_KV_HEADS * VT_ROWS, past), per_b3),
            pl.BlockSpec((None, past, KV_W), per_b3),
            pl.BlockSpec((None, N_KV_HEADS * VT_ROWS, past), per_b3),
        ],
        out_specs=pl.BlockSpec((tq, 512), lambda b, i: (b * nq + i, 0)),
        out_shape=jax.ShapeDtypeStruct((batch * seq, 512), BF16),
        scratch_shapes=[
            pltpu.VMEM((2, 2, N_KV_HEADS, Q_PER_KV * tq, KV_W), BF16),
            pltpu.VMEM((2, N_KV_HEADS, 1, Q_PER_KV * tq), F32),
            pltpu.VMEM((2, N_KV_HEADS, VT_ROWS, Q_PER_KV * tq), F32),
            pltpu.VMEM((N_KV_HEADS, kc, Q_PER_KV * tq), F32),
            pltpu.VMEM((N_KV_HEADS, kc, Q_PER_KV * tq), F32),
        ],
        compiler_params=_cparams(("parallel", "parallel")),
        name="attn_lat",
    )(sink, jnp.zeros((1,), jnp.int32), qkv, qkv, vt, kc_s, vc_s, kc_a, vc_a)


def _ssm_kernel(u_ref, bm_ref, cm_ref, a_ref, s0_ref, y_ref, fin_ref, bu_scr, y_scr, st_scr,
                *, lc, n_sub, reverse):
    n = pl.program_id(1)

    @pl.when(n == 0)
    def _():
        st_scr[...] = s0_ref[...]

    steps = lc // n_sub
    rows = steps * SUBLANES
    ar = jnp.broadcast_to(a_ref[0:1, :], (SUBLANES, N_STATE))
    ai = jnp.broadcast_to(a_ref[1:2, :], (SUBLANES, N_STATE))

    def project_in(i):
        u = u_ref[i * steps:(i + 1) * steps].reshape(rows, W_GRP).astype(BF16)
        bu_scr[i * rows:(i + 1) * rows, :] = jnp.dot(u, bm_ref[...], preferred_element_type=F32)

    def recurrence(i, carry):
        sr, si = carry
        ts = range(steps - 1, -1, -1) if reverse else range(steps)
        for t in ts:
            r = slice(i * rows + t * SUBLANES, i * rows + (t + 1) * SUBLANES)
            nr = ar * sr - ai * si + bu_scr[r, 0:N_STATE]
            ni = ar * si + ai * sr + bu_scr[r, N_STATE:2 * N_STATE]
            bu_scr[r, 0:N_STATE] = nr
            bu_scr[r, N_STATE:2 * N_STATE] = ni
            sr, si = nr, ni
        return sr, si

    def project_out(i):
        s = bu_scr[i * rows:(i + 1) * rows, :].astype(BF16)
        y = jnp.dot(s, cm_ref[...], preferred_element_type=F32)
        for half in range(W_GRP // LANES):
            y_scr[half, i * rows:(i + 1) * rows, :] = y[:, half * LANES:(half + 1) * LANES]

    order = list(range(n_sub - 1, -1, -1) if reverse else range(n_sub))
    carry = (st_scr[:, 0:N_STATE], st_scr[:, N_STATE:2 * N_STATE])
    project_in(order[0])
    for j, i in enumerate(order):
        if j + 1 < n_sub:
            project_in(order[j + 1])
        carry = recurrence(i, carry)
        project_out(i)
    st_scr[:, 0:N_STATE] = carry[0]
    st_scr[:, N_STATE:2 * N_STATE] = carry[1]
    for b in range(SUBLANES):
        for half in range(W_GRP // LANES):
            c0 = b * W_GRP + half * LANES
            y_ref[:, c0:c0 + LANES] = y_scr[half, pl.ds(b, lc, stride=SUBLANES), :]

    @pl.when(n == pl.num_programs(1) - 1)
    def _():
        fin_ref[...] = st_scr[...]


def _ssm(u_t, bmat, cmat, a, s0, *, batch, seq, lc, reverse):
    nch = seq // lc
    chunk = (lambda n: nch - 1 - n) if reverse else (lambda n: n)
    const = lambda b, n: (0, 0)
    return pl.pallas_call(
        functools.partial(_ssm_kernel, lc=lc, n_sub=4, reverse=reverse),
        grid=(batch // SUBLANES, nch),
        in_specs=[
            pl.BlockSpec((lc, SUBLANES, W_GRP), lambda b, n: (chunk(n), b, 0)),
            pl.BlockSpec((W_GRP, 2 * N_STATE), const),
            pl.BlockSpec((2 * N_STATE, W_GRP), const),
            pl.BlockSpec((2, N_STATE), const),
            pl.BlockSpec((SUBLANES, 2 * N_STATE), lambda b, n: (b, 0)),
        ],
        out_specs=[
            pl.BlockSpec((lc, SUBLANES * W_GRP), lambda b, n: (chunk(n), b)),
            pl.BlockSpec((SUBLANES, 2 * N_STATE), lambda b, n: (b, 0)),
        ],
        out_shape=[
            jax.ShapeDtypeStruct((seq, batch * W_GRP), F32),
            jax.ShapeDtypeStruct((batch, 2 * N_STATE), F32),
        ],
        scratch_shapes=[
            pltpu.VMEM((lc * SUBLANES, 2 * N_STATE), F32),
            pltpu.VMEM((W_GRP // LANES, lc * SUBLANES, LANES), F32),
            pltpu.VMEM((SUBLANES, 2 * N_STATE), F32),
        ],
        compiler_params=_cparams(("parallel", "arbitrary")),
        name="ssm_scan",
    )(u_t, bmat, cmat, a, s0)


def _dft_kernel(t1c_ref, t1s_ref, t2c_ref, t2s_ref, pc_ref, ps_ref, o_ref, lc_scr, ls_scr, acc_scr,
                *, tm, scale):
    kb = pl.program_id(2)
    t2c = t2c_ref[...]
    t2s = t2s_ref[...]
    for a in range(tm // DFT_RADIX):
        c1 = t1c_ref[a:a + 1, :]
        s1 = t1s_ref[a:a + 1, :]
        rows = slice(a * DFT_RADIX, (a + 1) * DFT_RADIX)
        lc_scr[rows, :] = (c1 * t2c - s1 * t2s).astype(BF16)
        ls_scr[rows, :] = (s1 * t2c + c1 * t2s).astype(BF16)
    part = (jnp.dot(lc_scr[...], pc_ref[...], preferred_element_type=F32)
            - jnp.dot(ls_scr[...], ps_ref[...], preferred_element_type=F32))

    @pl.when(kb == 0)
    def _():
        acc_scr[...] = part

    @pl.when(kb > 0)
    def _():
        acc_scr[...] += part

    @pl.when(kb == pl.num_programs(2) - 1)
    def _():
        o_ref[...] = (acc_scr[...] * scale).astype(BF16)


def _dft(tabs, pc, ps, *, seq, tm, tk, tn):
    nc = pc.shape[1]
    t1c, t1s, t2c, t2s = tabs
    ra = tm // DFT_RADIX
    scale = 1.0 / math.sqrt(seq * W_GRP)
    return pl.pallas_call(
        functools.partial(_dft_kernel, tm=tm, scale=scale),
        grid=(seq // tm, nc // tn, seq // tk),
        in_specs=[
            pl.BlockSpec((ra, tk), lambda i, j, k: (i, k)),
            pl.BlockSpec((ra, tk), lambda i, j, k: (i, k)),
            pl.BlockSpec((DFT_RADIX, tk), lambda i, j, k: (0, k)),
            pl.BlockSpec((DFT_RADIX, tk), lambda i, j, k: (0, k)),
            pl.BlockSpec((tk, tn), lambda i, j, k: (k, j)),
            pl.BlockSpec((tk, tn), lambda i, j, k: (k, j)),
        ],
        out_specs=pl.BlockSpec((tm, tn), lambda i, j, k: (i, j)),
        out_shape=jax.ShapeDtypeStruct((seq, nc), BF16),
        scratch_shapes=[pltpu.VMEM((tm, tk), BF16), pltpu.VMEM((tm, tk), BF16), pltpu.VMEM((tm, tn), F32)],
        compiler_params=_cparams(("parallel", "parallel", "arbitrary")),
        name="pos_dft",
    )(t1c, t1s, t2c, t2s, pc, ps)


def _mix_out_kernel(h_ref, mod_ref, yf_ref, yb_ref, u_ref, att_ref, f_ref, d_ref, wglu_ref, bglu_ref,
                    wf_ref, bf_ref, wo_ref, o_ref, m_scr):
    y = d_ref[...] * u_ref[...] + yf_ref[...] + yb_ref[...]
    y = y * (0.5 * (1.0 + jnp.tanh(math.sqrt(2.0 / math.pi) * (y + 0.044715 * (y * y * y)))))
    z = jnp.dot(y.astype(BF16), wglu_ref[...], preferred_element_type=F32) + bglu_ref[...]
    m_scr[:, 0:256] = (y * jax.nn.sigmoid(z)).astype(BF16)
    m_scr[:, 256:768] = att_ref[...]
    m_scr[:, 768:1024] = (jnp.dot(f_ref[...], wf_ref[...], preferred_element_type=F32) + bf_ref[...]).astype(BF16)
    mixed = jnp.dot(m_scr[...], wo_ref[...], preferred_element_type=F32)
    o_ref[...] = h_ref[...] + mod_ref[0, 5:6, :] * mixed


def _mix_out(h, mod, y_fwd, y_bwd, u_t, att, f, d, wglu, bglu, wf, bfn, wo, *, batch, seq, tm, cond_base,
             per_batch_cond):
    t = batch * seq
    tiles = seq // tm
    cond = lambda i: (cond_base + (i // tiles if per_batch_cond else 0), 0, 0)
    tcol = lambda i: (i % tiles, i // tiles)
    const = lambda i: (0, 0)
    return pl.pallas_call(
        _mix_out_kernel,
        grid=(t // tm,),
        in_specs=[
            pl.BlockSpec((tm, D_MODEL), lambda i: (i, 0)),
            pl.BlockSpec((1, N_MOD, D_MODEL), cond),
            pl.BlockSpec((tm, W_GRP), tcol),
            pl.BlockSpec((tm, W_GRP), tcol),
            pl.BlockSpec((tm, W_GRP), tcol),
            pl.BlockSpec((tm, 512), lambda i: (i, 0)),
            pl.BlockSpec((tm, W_GRP), tcol),
            pl.BlockSpec((1, W_GRP), const),
            pl.BlockSpec((W_GRP, W_GRP), const),
            pl.BlockSpec((1, W_GRP), const),
            pl.BlockSpec((W_GRP, W_GRP), const),
            pl.BlockSpec((1, W_GRP), const),
            pl.BlockSpec((D_MODEL, D_MODEL), const),
        ],
        out_specs=pl.BlockSpec((tm, D_MODEL), lambda i: (i, 0)),
        out_shape=jax.ShapeDtypeStruct((t, D_MODEL), F32),
        scratch_shapes=[pltpu.VMEM((tm, D_MODEL), BF16)],
        compiler_params=_cparams(("parallel",)),
        name="mix_out",
    )(h, mod, y_fwd, y_bwd, u_t, att, f, d, wglu, bglu, wf, bfn, wo)


def _rope_tables(seq):
    rows = seq // GRID_W
    row_id = jnp.repeat(jnp.arange(rows), GRID_W).astype(F32)
    col_id = jnp.tile(jnp.arange(GRID_W), rows).astype(F32)
    n_freq = HEAD_DIM // 4
    inv = ROPE_BASE ** (-jnp.arange(n_freq, dtype=F32) / n_freq)
    ang = jnp.concatenate([row_id[:, None] * inv, col_id[:, None] * inv], axis=-1)
    cos, sin = jnp.cos(ang), jnp.sin(ang)
    cos_full = jnp.tile(jnp.concatenate([cos, cos], axis=-1), (1, N_Q_HEADS))
    sin_signed = jnp.tile(jnp.concatenate([-sin, sin], axis=-1), (1, N_Q_HEADS))
    return cos_full, sin_signed


def _angle_table(mult, n, period):
    m = (mult[:, None] * n[None, :]) % period
    th = m.astype(F32) * (2.0 * math.pi / period)
    return jnp.cos(th), jnp.sin(th)


def _dft_tables(seq):
    n = jnp.arange(seq, dtype=jnp.int32)
    t1c, t1s = _angle_table(DFT_RADIX * jnp.arange(seq // DFT_RADIX, dtype=jnp.int32), n, seq)
    t2c, t2s = _angle_table(jnp.arange(DFT_RADIX, dtype=jnp.int32), n, seq)
    return t1c, t1s, t2c, t2s


def _ssm_params(lam_re, lam_im, b_re, b_im, c_re, c_im, log_dt):
    dt = jnp.exp(log_dt)[..., None]
    mag = jnp.exp(lam_re * dt)
    ar = mag * jnp.cos(lam_im * dt)
    ai = mag * jnp.sin(lam_im * dt)
    den = lam_re * lam_re + lam_im * lam_im
    qr = ((ar - 1.0) * lam_re + ai * lam_im) / den
    qi = (ai * lam_re - (ar - 1.0) * lam_im) / den
    bb_re = qr[..., None] * b_re - qi[..., None] * b_im
    bb_im = qr[..., None] * b_im + qi[..., None] * b_re
    eye = jnp.eye(N_SSM_GROUPS, dtype=F32)

    def block_diag(x):
        y = jnp.transpose(x, (0, 1, 3, 2))[:, :, :, None, :] * eye[None, :, None, :, None]
        return y.reshape(2, x.shape[1] * x.shape[3], x.shape[1] * x.shape[2])

    bmat = jnp.concatenate([block_diag(bb_re), block_diag(bb_im)], axis=-1).astype(BF16)
    cmat = jnp.concatenate([block_diag(c_re), block_diag(-c_im)], axis=1).astype(BF16)
    a2 = jnp.stack([ar.reshape(2, N_STATE), ai.reshape(2, N_STATE)], axis=1)
    return a2, bmat, cmat


def _permute_q_heads(w_in):
    cols = jnp.arange(P_IN, dtype=jnp.int32)
    for base in (256, 768):
        blk = jnp.concatenate([base + h * HEAD_DIM + jnp.arange(HEAD_DIM, dtype=jnp.int32) for h in Q_HEAD_ORDER])
        cols = cols.at[base:base + N_Q_HEADS * HEAD_DIM].set(blk)
    return jnp.take(w_in, cols, axis=1)


def _cache_layout(cache):
    b, _, past, nkv, dh = cache.shape
    keys = cache[:, 0].reshape(b, past, nkv * dh).astype(BF16)
    vt = jnp.transpose(cache[:, 1], (0, 2, 3, 1))
    ones = jnp.ones((b, nkv, VT_ROWS - dh, past), vt.dtype)
    vt = jnp.concatenate([vt, ones], axis=2).reshape(b, nkv * VT_ROWS, past).astype(BF16)
    return keys, vt


def kernel(x_prompt, x_sample, cache_swa_kv, cache_axial_kv, state_ssm, c, c_ctx, w_mod, b_mod, norm_ffn1, norm_mix, norm_ffn2, ffn1_w_gate, ffn1_w_up, ffn1_w_down, ffn2_w_gate, ffn2_w_up, ffn2_w_down, w_in, w_out, ssm_lambda_re, ssm_lambda_im, ssm_b_re, ssm_b_im, ssm_c_re, ssm_c_im, ssm_log_dt, ssm_d, ssm_w_glu, ssm_b_glu, swa_sink, ax_q_norm, ax_k_norm, fnet_w, fnet_b, final_norm):
    depth = w_mod.shape[0]
    cb, cl, _ = x_prompt.shape
    lb, ll, _ = x_sample.shape

    cond = jnp.zeros((16, D_MODEL), F32).at[0].set(c_ctx).at[1:1 + lb].set(c)
    mod_all = _modulation(cond, w_mod, b_mod.reshape(depth, 1, N_MOD * D_MODEL))
    mod_all = mod_all.reshape(depth, 16, N_MOD, D_MODEL)

    e_heads = jnp.kron(jnp.eye(256 // HEAD_DIM, dtype=F32), jnp.ones((HEAD_DIM, HEAD_DIM), F32)) / HEAD_DIM
    e_heads = e_heads.astype(BF16)
    kc = jnp.arange(W_GRP, dtype=jnp.int32)
    cc, sc = _angle_table(kc, kc, W_GRP)
    cs_chan = jnp.concatenate([cc, sc], axis=-1).astype(BF16)
    rope_tabs = _rope_tables(ll)
    dft_ctx = _dft_tables(cl)
    dft_lat = _dft_tables(ll)
    fg = final_norm.reshape(1, D_MODEL)

    h_ctx = x_prompt.reshape(cb * cl, D_MODEL)
    h_lat = x_sample.reshape(lb * ll, D_MODEL)
    tm_lat = 512
    swa_list, ax_list, ssm_list = [], [], []
    for l in range(depth):
        mod = mod_all[l]
        bf = lambda w: w[l].astype(BF16)
        row = lambda v: v[l].reshape(1, -1)
        ffn1 = (row(norm_ffn1), bf(ffn1_w_gate), bf(ffn1_w_up), bf(ffn1_w_down), fg)
        ffn2 = (row(norm_ffn2), bf(ffn2_w_gate), bf(ffn2_w_up), bf(ffn2_w_down), fg)
        w_in_l, w_out_l = _permute_q_heads(w_in[l]).astype(BF16), bf(w_out)
        qn = jnp.tile(ax_q_norm[l], N_Q_HEADS).reshape(1, 256)
        kn = jnp.tile(ax_k_norm[l], KV_W // HEAD_DIM).reshape(1, KV_W)
        a2, bmat, cmat = _ssm_params(ssm_lambda_re[l], ssm_lambda_im[l], ssm_b_re[l], ssm_b_im[l],
                                     ssm_c_re[l], ssm_c_im[l], ssm_log_dt[l])
        mix_out_w = (row(ssm_d), bf(ssm_w_glu), row(ssm_b_glu), bf(fnet_w), row(fnet_b), w_out_l)
        sink = swa_sink[l]
        last = l == depth - 1

        h_ctx = _ffn(h_ctx, mod, *ffn1, mod_base=0, tm=512, tiles_per_cond=1 << 30, cond_base=0, final_norm=False)
        u_t, qkv, pc, ps, kv_s, kv_a = _mix_in(h_ctx, mod, row(norm_mix), w_in_l, qn, kn, e_heads, cs_chan, None,
                                               batch=cb, seq=cl, tm=cl, cond_base=0, rope=False)
        u3 = u_t.reshape(cl, cb, W_GRP)
        zero_state = jnp.zeros((cb, 2 * N_STATE), F32)
        y_dirs, s_fins = zip(*[_ssm(u3, bmat[d], cmat[d], a2[d], zero_state, batch=cb, seq=cl, lc=128,
                                    reverse=bool(d)) for d in range(2)])
        att = _attn_ctx(sink, qkv, batch=cb, seq=cl)
        f = _dft(dft_ctx, pc, ps, seq=cl, tm=cl, tk=cl, tn=2048)
        h_ctx = _mix_out(h_ctx, mod, *y_dirs, u_t, att, f, *mix_out_w,
                         batch=cb, seq=cl, tm=cl, cond_base=0, per_batch_cond=False)
        h_ctx = _ffn(h_ctx, mod, *ffn2, mod_base=6, tm=512, tiles_per_cond=1 << 30, cond_base=0, final_norm=last)
        swa_list.append(kv_s.reshape(cb, 2, cl, KV_W // HEAD_DIM, HEAD_DIM))
        ax_list.append(kv_a.reshape(cb, 2, cl, KV_W // HEAD_DIM, HEAD_DIM))
        s_fin = jnp.stack(s_fins, axis=0).reshape(2, cb, 2, N_SSM_GROUPS, SSM_STATE)
        ssm_list.append(jnp.transpose(s_fin, (1, 0, 3, 4, 2)))

        tpc = ll // tm_lat
        h_lat = _ffn(h_lat, mod, *ffn1, mod_base=0, tm=tm_lat, tiles_per_cond=tpc, cond_base=1, final_norm=False)
        u_t, qkv, pc, ps, vt = _mix_in(h_lat, mod, row(norm_mix), w_in_l, qn, kn, e_heads, cs_chan, rope_tabs,
                                       batch=lb, seq=ll, tm=tm_lat, cond_base=1, rope=True)
        s0 = jnp.transpose(state_ssm[:, l], (1, 0, 4, 2, 3)).reshape(2, lb, 2 * N_STATE)
        u3 = u_t.reshape(ll, lb, W_GRP)
        y_dirs = [_ssm(u3, bmat[d], cmat[d], a2[d], s0[d], batch=lb, seq=ll, lc=128, reverse=bool(d))[0]
                  for d in range(2)]
        kc_s, vc_s = _cache_layout(cache_swa_kv[:, l])
        kc_a, vc_a = _cache_layout(cache_axial_kv[:, l])
        att = _attn_lat(sink, qkv, vt, kc_s, vc_s, kc_a, vc_a, batch=lb, seq=ll, tq=256)
        f = _dft(dft_lat, pc, ps, seq=ll, tm=512, tk=1024, tn=2048)
        h_lat = _mix_out(h_lat, mod, *y_dirs, u_t, att, f, *mix_out_w,
                         batch=lb, seq=ll, tm=tm_lat, cond_base=1, per_batch_cond=True)
        h_lat = _ffn(h_lat, mod, *ffn2, mod_base=6, tm=tm_lat, tiles_per_cond=tpc, cond_base=1, final_norm=last)

    y_prompt = h_ctx.reshape(cb, cl, D_MODEL)
    y_sample = h_lat.reshape(lb, ll, D_MODEL)
    return (y_prompt, y_sample, jnp.stack(swa_list, axis=1), jnp.stack(ax_list, axis=1),
            jnp.stack(ssm_list, axis=1))
```

```python
import functools
import math

import jax
import jax.numpy as jnp
from jax import lax
from jax.experimental import pallas as pl
from jax.experimental.pallas import tpu as pltpu

F32 = jnp.float32
BF16 = jnp.bfloat16

D_MODEL = 1024
D_FF = 2816
N_MOD = 9
EPS = 1e-6
HEAD_DIM = 64
HALF_HEAD = HEAD_DIM // 2
N_Q_HEADS = 4
Q_PER_KV = 2
W_GRP = 256
KV_W = 128
P_IN = 1536
N_SSM_GROUPS = 16
SSM_GROUP = 16
SSM_STATE = 64
N_STATE = N_SSM_GROUPS * SSM_STATE
WINDOW = 128
GRID_W = 64
ROPE_BASE = 10000.0
NEG_BIG = -1e30
LOG2E = math.log2(math.e)
Q_HEAD_ORDER = (0, 2, 1, 3)
N_KV_HEADS = N_Q_HEADS // Q_PER_KV
VT_ROWS = 80

FF_CHUNKS = (1024, 1024, 768)
SUBLANES = 8
LANES = 128
DFT_RADIX = 64
VMEM_LIMIT = 56 * 1024 * 1024


def _cparams(sem):
    return pltpu.CompilerParams(dimension_semantics=sem, vmem_limit_bytes=VMEM_LIMIT)


def _adaln(h, g, sc, sh):
    ms = jnp.mean(h * h, axis=-1, keepdims=True)
    return (h * lax.rsqrt(ms + EPS) * g) * (1.0 + sc) + sh


def _mod_kernel(c_ref, w_ref, b_ref, o_ref):
    c = c_ref[...]
    a = (c * jax.nn.sigmoid(c)).astype(BF16)
    o_ref[...] = jnp.dot(a, w_ref[...].astype(BF16), preferred_element_type=F32) + b_ref[...]


def _modulation(cond, w_mod, b_mod):
    depth = w_mod.shape[0]
    n = N_MOD * D_MODEL
    tn = 1024
    return pl.pallas_call(
        _mod_kernel,
        grid=(depth, n // tn),
        in_specs=[
            pl.BlockSpec((16, D_MODEL), lambda l, j: (0, 0)),
            pl.BlockSpec((None, D_MODEL, tn), lambda l, j: (l, 0, j)),
            pl.BlockSpec((None, 1, tn), lambda l, j: (l, 0, j)),
        ],
        out_specs=pl.BlockSpec((None, 16, tn), lambda l, j: (l, 0, j)),
        out_shape=jax.ShapeDtypeStruct((depth, 16, n), F32),
        compiler_params=_cparams(("parallel", "parallel")),
        name="modulation",
    )(cond, w_mod, b_mod)


def _ffn_kernel(h_ref, mod_ref, g_ref, wg_ref, wu_ref, wd_ref, fg_ref, o_ref, *, mod_base, final_norm):
    h = h_ref[...]
    x = _adaln(h, g_ref[...], mod_ref[0, mod_base + 1:mod_base + 2, :],
               mod_ref[0, mod_base:mod_base + 1, :]).astype(BF16)
    acc = None
    c0 = 0
    for width in FF_CHUNKS:
        gate = jnp.dot(x, wg_ref[:, c0:c0 + width], preferred_element_type=F32)
        up = jnp.dot(x, wu_ref[:, c0:c0 + width], preferred_element_type=F32)
        hg = 0.5 * gate
        a = (hg * (1.0 + jnp.tanh(hg)) * up).astype(BF16)
        part = jnp.dot(a, wd_ref[c0:c0 + width, :], preferred_element_type=F32)
        acc = part if acc is None else acc + part
        c0 += width
    hn = h + (0.5 * mod_ref[0, mod_base + 2:mod_base + 3, :]) * acc
    if final_norm:
        ms = jnp.mean(hn * hn, axis=-1, keepdims=True)
        hn = hn * lax.rsqrt(ms + EPS) * fg_ref[...]
    o_ref[...] = hn


def _ffn(h, mod, g, wg, wu, wd, fg, *, mod_base, tm, tiles_per_cond, cond_base, final_norm):
    t = h.shape[0]
    cond = lambda i: (cond_base + i // tiles_per_cond, 0, 0)
    const = lambda i: (0, 0)
    resident = dict(pipeline_mode=pl.Buffered(1))
    kern = functools.partial(_ffn_kernel, mod_base=mod_base, final_norm=final_norm)
    return pl.pallas_call(
        kern,
        grid=(t // tm,),
        in_specs=[
            pl.BlockSpec((tm, D_MODEL), lambda i: (i, 0)),
            pl.BlockSpec((1, N_MOD, D_MODEL), cond),
            pl.BlockSpec((1, D_MODEL), const),
            pl.BlockSpec((D_MODEL, D_FF), const, **resident),
            pl.BlockSpec((D_MODEL, D_FF), const, **resident),
            pl.BlockSpec((D_FF, D_MODEL), const, **resident),
            pl.BlockSpec((1, D_MODEL), const),
        ],
        out_specs=pl.BlockSpec((tm, D_MODEL), lambda i: (i, 0)),
        out_shape=jax.ShapeDtypeStruct((t, D_MODEL), F32),
        compiler_params=_cparams(("parallel",)),
        name="ffn",
    )(h, mod, g, wg, wu, wd, fg)


def _head_mean_sq(x, e):
    x2 = x * x
    hi = x2.astype(BF16)
    lo = (x2 - hi.astype(F32)).astype(BF16)
    return jnp.dot(hi, e, preferred_element_type=F32) + jnp.dot(lo, e, preferred_element_type=F32)


def _swap_halves(x):
    w = x.shape[-1]
    lane = lax.broadcasted_iota(jnp.int32, x.shape, 1)
    first = (lane & HALF_HEAD) == 0
    return jnp.where(first, pltpu.roll(x, w - HALF_HEAD, 1), pltpu.roll(x, HALF_HEAD, 1))


def _mix_in_kernel(*refs, rope):
    if rope:
        (h_ref, mod_ref, g_ref, w_ref, qn_ref, kn_ref, e_ref, cs_ref, cos_ref, sin_ref,
         u_ref, qkv_ref, pc_ref, ps_ref, vt_ref) = refs
    else:
        (h_ref, mod_ref, g_ref, w_ref, qn_ref, kn_ref, e_ref, cs_ref,
         u_ref, qkv_ref, pc_ref, ps_ref, kvs_ref, kva_ref) = refs
    x = _adaln(h_ref[...], g_ref[...], mod_ref[0, 4:5, :], mod_ref[0, 3:4, :]).astype(BF16)
    proj = jnp.dot(x, w_ref[...], preferred_element_type=F32)
    u_ssm = proj[:, 0:256]
    q_s = proj[:, 256:512]
    k_s = proj[:, 512:640]
    v_s = proj[:, 640:768]
    q_a = proj[:, 768:1024]
    k_a = proj[:, 1024:1152]
    v_a = proj[:, 1152:1280]
    u_f = proj[:, 1280:1536]

    e = e_ref[...]
    q_a = q_a * lax.rsqrt(_head_mean_sq(q_a, e) + EPS) * qn_ref[...]
    k_a = k_a * lax.rsqrt(_head_mean_sq(k_a, e[0:KV_W, 0:KV_W]) + EPS) * kn_ref[...]

    u_ref[...] = u_ssm
    p = jnp.dot(u_f.astype(BF16), cs_ref[...], preferred_element_type=F32)
    pc_ref[...] = p[:, 0:256].astype(BF16)
    ps_ref[...] = p[:, 256:512].astype(BF16)

    if rope:
        scale = LOG2E * HEAD_DIM ** -0.5
        cos = cos_ref[...]
        sin = sin_ref[...]
        rot = lambda t, w: t * cos[:, 0:w] + _swap_halves(t) * sin[:, 0:w]
        qkv_ref[:, 0:256] = (q_s * scale).astype(BF16)
        qkv_ref[:, 256:512] = (rot(q_s, 256) * scale).astype(BF16)
        qkv_ref[:, 512:768] = (q_a * scale).astype(BF16)
        qkv_ref[:, 768:1024] = (rot(q_a, 256) * scale).astype(BF16)
        qkv_ref[:, 1024:1152] = rot(k_s, KV_W).astype(BF16)
        qkv_ref[:, 1152:1280] = rot(k_a, KV_W).astype(BF16)
        ones = jnp.ones((VT_ROWS - HEAD_DIM, v_s.shape[0]), BF16)
        for grp, v in enumerate((v_s, v_a)):
            vt = jnp.transpose(v).astype(BF16)
            for kv in range(N_KV_HEADS):
                r0 = (grp * N_KV_HEADS + kv) * VT_ROWS
                vt_ref[r0:r0 + HEAD_DIM, :] = vt[kv * HEAD_DIM:(kv + 1) * HEAD_DIM, :]
                vt_ref[r0 + HEAD_DIM:r0 + VT_ROWS, :] = ones
    else:
        scale = HEAD_DIM ** -0.5
        qkv_ref[:, 0:256] = (q_s * scale).astype(BF16)
        qkv_ref[:, 256:384] = k_s.astype(BF16)
        qkv_ref[:, 384:512] = v_s.astype(BF16)
        qkv_ref[:, 512:768] = (q_a * scale).astype(BF16)
        qkv_ref[:, 768:896] = k_a.astype(BF16)
        qkv_ref[:, 896:1024] = v_a.astype(BF16)
        kvs_ref[0] = k_s
        kvs_ref[1] = v_s
        kva_ref[0] = k_a
        kva_ref[1] = v_a


def _mix_in(h, mod, g, w_in, qn, kn, e, cs, rope_tabs, *, batch, seq, tm, cond_base, rope):
    t = batch * seq
    tiles = seq // tm
    nc = batch * W_GRP
    cond = lambda i: (cond_base + (i // tiles if rope else 0), 0, 0)
    tcol = lambda i: (i % tiles, i // tiles)
    const = lambda i: (0, 0)
    in_specs = [
        pl.BlockSpec((tm, D_MODEL), lambda i: (i, 0)),
        pl.BlockSpec((1, N_MOD, D_MODEL), cond),
        pl.BlockSpec((1, D_MODEL), const),
        pl.BlockSpec((D_MODEL, P_IN), const),
        pl.BlockSpec((1, 256), const),
        pl.BlockSpec((1, KV_W), const),
        pl.BlockSpec((256, 256), const),
        pl.BlockSpec((256, 512), const),
    ]
    args = [h, mod, g, w_in, qn, kn, e, cs]
    qkv_w = 1280 if rope else 1024
    out_specs = [
        pl.BlockSpec((tm, W_GRP), tcol),
        pl.BlockSpec((tm, qkv_w), lambda i: (i, 0)),
        pl.BlockSpec((tm, W_GRP), tcol),
        pl.BlockSpec((tm, W_GRP), tcol),
    ]
    out_shape = [
        jax.ShapeDtypeStruct((seq, nc), F32),
        jax.ShapeDtypeStruct((t, qkv_w), BF16),
        jax.ShapeDtypeStruct((seq, nc), BF16),
        jax.ShapeDtypeStruct((seq, nc), BF16),
    ]
    if rope:
        in_specs += [pl.BlockSpec((tm, 256), lambda i: (i % tiles, 0))] * 2
        args += list(rope_tabs)
        out_specs.append(pl.BlockSpec((None, 2 * N_KV_HEADS * VT_ROWS, tm), lambda i: (i // tiles, 0, i % tiles)))
        out_shape.append(jax.ShapeDtypeStruct((batch, 2 * N_KV_HEADS * VT_ROWS, seq), BF16))
    else:
        assert tm == seq
        out_specs += [pl.BlockSpec((None, 2, seq, KV_W), lambda i: (i, 0, 0, 0))] * 2
        out_shape += [jax.ShapeDtypeStruct((batch, 2, seq, KV_W), F32)] * 2
    return pl.pallas_call(
        functools.partial(_mix_in_kernel, rope=rope),
        grid=(t // tm,),
        in_specs=in_specs,
        out_specs=out_specs,
        out_shape=out_shape,
        compiler_params=_cparams(("parallel",)),
        name="mix_in",
    )(*args)


def _qk(a, b):
    return lax.dot_general(a, b, (((1,), (1,)), ((), ())), preferred_element_type=F32)


def _attn_ctx_kernel(sink_ref, qkv_ref, o_ref):
    for grp in range(2):
        base = grp * 512
        for pos, h in enumerate(Q_HEAD_ORDER):
            kv = h // Q_PER_KV
            q = qkv_ref[:, base + pos * HEAD_DIM:base + (pos + 1) * HEAD_DIM]
            k = qkv_ref[:, base + 256 + kv * HEAD_DIM:base + 256 + (kv + 1) * HEAD_DIM]
            v = qkv_ref[:, base + 384 + kv * HEAD_DIM:base + 384 + (kv + 1) * HEAD_DIM]
            s = _qk(q, k)
            m = jnp.max(s, axis=-1, keepdims=True)
            if grp == 0:
                m = jnp.maximum(m, sink_ref[h])
            p = jnp.exp(s - m)
            l = jnp.sum(p, axis=-1, keepdims=True)
            if grp == 0:
                l = l + jnp.exp(sink_ref[h] - m)
            o = jnp.dot(p.astype(BF16), v, preferred_element_type=F32) / l
            o_ref[:, grp * 256 + h * HEAD_DIM:grp * 256 + (h + 1) * HEAD_DIM] = o.astype(BF16)


def _attn_ctx(sink, qkv, *, batch, seq):
    return pl.pallas_call(
        _attn_ctx_kernel,
        grid=(batch,),
        in_specs=[
            pl.BlockSpec(memory_space=pltpu.SMEM),
            pl.BlockSpec((seq, 1024), lambda b: (b, 0)),
        ],
        out_specs=pl.BlockSpec((seq, 512), lambda b: (b, 0)),
        out_shape=jax.ShapeDtypeStruct((batch * seq, 512), BF16),
        compiler_params=_cparams(("parallel",)),
        name="attn_ctx",
    )(sink, qkv)


def _attn_lat_kernel(sink_ref, zero_ref, q_ref, k_ref, vt_ref, kcs_ref, vcs_ref, kca_ref, vca_ref, o_ref,
                     qm_scr, m_scr, acc_scr, s0_scr, s1_scr, bm_scr, *, tq, seq, kc):
    i = pl.program_id(1)
    q0 = i * tq
    win = tq + 2 * WINDOW
    past = kcs_ref.shape[0]
    assert win <= kc and past <= kc
    ws = pl.multiple_of(jnp.clip(q0 - WINDOW, 0, seq - win), WINDOW)
    lane = lax.broadcasted_iota(jnp.int32, (Q_PER_KV * tq, KV_W), 1)

    def banded(s):
        kpos = ws + lax.broadcasted_iota(jnp.int32, s.shape, 0)
        qpos = q0 + (lax.broadcasted_iota(jnp.int32, s.shape, 1) & (tq - 1))
        return jnp.where(jnp.abs(qpos - kpos) <= WINDOW, s, NEG_BIG)

    def sink_row(kv):
        return jnp.concatenate(
            [jnp.full((1, tq), sink_ref[Q_PER_KV * kv + j] * LOG2E, F32) for j in range(Q_PER_KV)], axis=1)

    def vt_rows(grp, kv):
        r0 = (grp * N_KV_HEADS + kv) * VT_ROWS
        return slice(r0, r0 + VT_ROWS)

    acc_scr[...] = jnp.zeros(acc_scr.shape, F32)
    for grp in range(2):
        base = grp * 512
        q_plain = jnp.concatenate([q_ref[:, base:base + KV_W], q_ref[:, base + KV_W:base + 2 * KV_W]], axis=0)
        q_rope = jnp.concatenate([q_ref[:, base + 256:base + 256 + KV_W],
                                  q_ref[:, base + 256 + KV_W:base + 256 + 2 * KV_W]], axis=0)
        for kv in range(N_KV_HEADS):
            keep = (lane < HEAD_DIM) if kv == 0 else (lane >= HEAD_DIM)
            qm_scr[grp, 0, kv] = jnp.where(keep, q_plain, 0)
            qm_scr[grp, 1, kv] = jnp.where(keep, q_rope, 0)
            m_scr[grp, kv] = sink_row(kv) if grp == 0 else jnp.full((1, Q_PER_KV * tq), NEG_BIG, F32)

    jobs = [
        (0, past, lambda kv: _qk(kcs_ref[...], qm_scr[0, 0, kv]),
         lambda kv: vcs_ref[kv * VT_ROWS:(kv + 1) * VT_ROWS, :]),
        (0, win, lambda kv: banded(_qk(k_ref[pl.ds(ws, win), 0:KV_W], qm_scr[0, 1, kv])),
         lambda kv: vt_ref[vt_rows(0, kv), pl.ds(ws, win)]),
        (1, past, lambda kv: _qk(kca_ref[...], qm_scr[1, 0, kv]),
         lambda kv: vca_ref[kv * VT_ROWS:(kv + 1) * VT_ROWS, :]),
    ]
    for c in range(seq // kc):
        jobs.append((1, kc, lambda kv, c=c: _qk(k_ref[c * kc:(c + 1) * kc, KV_W:2 * KV_W], qm_scr[1, 1, kv]),
                     lambda kv, c=c: vt_ref[vt_rows(1, kv), c * kc:(c + 1) * kc]))

    s_slots = (s0_scr, s1_scr)
    row0 = pl.multiple_of(zero_ref[0], SUBLANES)

    def scores_to(slot, job):
        _, n, scores, _ = job
        for kv in range(N_KV_HEADS):
            s = scores(kv)
            s_slots[slot][kv, 0:n, :] = s
            bm_scr[slot, kv] = jnp.max(s, axis=0, keepdims=True)

    def update_from(slot, job):
        grp, n, _, values = job
        for kv in range(N_KV_HEADS):
            m_old = m_scr[grp, kv]
            m_new = jnp.maximum(m_old, bm_scr[slot, kv])
            p = jnp.exp2(s_slots[slot][kv, pl.ds(row0, n), :] - m_new).astype(BF16)
            acc_scr[grp, kv] = (jnp.exp2(m_old - m_new) * acc_scr[grp, kv]
                                + jnp.dot(values(kv), p, preferred_element_type=F32))
            m_scr[grp, kv] = m_new

    def finalize(grp):
        outs = []
        for kv in range(N_KV_HEADS):
            acc = acc_scr[grp, kv]
            den = acc[HEAD_DIM:HEAD_DIM + 1, :]
            if grp == 0:
                den = den + jnp.exp2(sink_row(kv) - m_scr[grp, kv])
            o = acc[0:HEAD_DIM, :] / den
            outs += [o[:, j * tq:(j + 1) * tq] for j in range(Q_PER_KV)]
        o_t = jnp.concatenate(outs, axis=0)
        o_ref[:, grp * 256:(grp + 1) * 256] = jnp.transpose(o_t).astype(BF16)

    scores_to(0, jobs[0])
    for j, job in enumerate(jobs):
        if j + 1 < len(jobs):
            scores_to((j + 1) % 2, jobs[j + 1])
        update_from(j % 2, job)
        if j + 1 == len(jobs) or jobs[j + 1][0] != job[0]:
            finalize(job[0])


def _attn_lat(sink, qkv, vt, kc_s, vc_s, kc_a, vc_a, *, batch, seq, tq, kc=512):
    nq = seq // tq
    past = kc_s.shape[1]
    per_b3 = lambda b, i: (b, 0, 0)
    return pl.pallas_call(
        functools.partial(_attn_lat_kernel, tq=tq, seq=seq, kc=kc),
        grid=(batch, nq),
        in_specs=[
            pl.BlockSpec(memory_space=pltpu.SMEM),
            pl.BlockSpec(memory_space=pltpu.SMEM),
            pl.BlockSpec((tq, 1024), lambda b, i: (b * nq + i, 0)),
            pl.BlockSpec((seq, 256), lambda b, i: (b, 4)),
            pl.BlockSpec((None, 2 * N_KV_HEADS * VT_ROWS, seq), per_b3),
            pl.BlockSpec((None, past, KV_W), per_b3),
            pl.BlockSpec((None, N_KV_HEADS * VT_ROWS, past), per_b3),
            pl.BlockSpec((None, past, KV_W), per_b3),
            pl.BlockSpec((None, N_KV_HEADS * VT_ROWS, past), per_b3),
        ],
        out_specs=pl.BlockSpec((tq, 512), lambda b, i: (b * nq + i, 0)),
        out_shape=jax.ShapeDtypeStruct((batch * seq, 512), BF16),
        scratch_shapes=[
            pltpu.VMEM((2, 2, N_KV_HEADS, Q_PER_KV * tq, KV_W), BF16),
            pltpu.VMEM((2, N_KV_HEADS, 1, Q_PER_KV * tq), F32),
            pltpu.VMEM((2, N_KV_HEADS, VT_ROWS, Q_PER_KV * tq), F32),
            pltpu.VMEM((N_KV_HEADS, kc, Q_PER_KV * tq), F32),
            pltpu.VMEM((N_KV_HEADS, kc, Q_PER_KV * tq), F32),
            pltpu.VMEM((2, N_KV_HEADS, 1, Q_PER_KV * tq), F32),
        ],
        compiler_params=_cparams(("parallel", "parallel")),
        name="attn_lat",
    )(sink, jnp.zeros((1,), jnp.int32), qkv, qkv, vt, kc_s, vc_s, kc_a, vc_a)


def _ssm_kernel(u_ref, bm_ref, cm_ref, a_ref, s0_ref, y_ref, fin_ref, bu_scr, y_scr, st_scr,
                *, lc, n_sub, reverse):
    n = pl.program_id(1)

    @pl.when(n == 0)
    def _():
        st_scr[...] = s0_ref[...]

    steps = lc // n_sub
    rows = steps * SUBLANES
    ar = jnp.broadcast_to(a_ref[0:1, :], (SUBLANES, N_STATE))
    ai = jnp.broadcast_to(a_ref[1:2, :], (SUBLANES, N_STATE))

    def project_in(i):
        u = u_ref[i * steps:(i + 1) * steps].reshape(rows, W_GRP).astype(BF16)
        bu_scr[i * rows:(i + 1) * rows, :] = jnp.dot(u, bm_ref[...], preferred_element_type=F32)

    def recurrence(i, carry):
        sr, si = carry
        ts = range(steps - 1, -1, -1) if reverse else range(steps)
        for t in ts:
            r = slice(i * rows + t * SUBLANES, i * rows + (t + 1) * SUBLANES)
            nr = ar * sr - ai * si + bu_scr[r, 0:N_STATE]
            ni = ar * si + ai * sr + bu_scr[r, N_STATE:2 * N_STATE]
            bu_scr[r, 0:N_STATE] = nr
            bu_scr[r, N_STATE:2 * N_STATE] = ni
            sr, si = nr, ni
        return sr, si

    def project_out(i):
        s = bu_scr[i * rows:(i + 1) * rows, :].astype(BF16)
        y = jnp.dot(s, cm_ref[...], preferred_element_type=F32)
        for half in range(W_GRP // LANES):
            y_scr[half, i * rows:(i + 1) * rows, :] = y[:, half * LANES:(half + 1) * LANES]

    order = list(range(n_sub - 1, -1, -1) if reverse else range(n_sub))
    carry = (st_scr[:, 0:N_STATE], st_scr[:, N_STATE:2 * N_STATE])
    project_in(order[0])
    for j, i in enumerate(order):
        if j + 1 < n_sub:
            project_in(order[j + 1])
        carry = recurrence(i, carry)
        project_out(i)
    st_scr[:, 0:N_STATE] = carry[0]
    st_scr[:, N_STATE:2 * N_STATE] = carry[1]
    for b in range(SUBLANES):
        for half in range(W_GRP // LANES):
            c0 = b * W_GRP + half * LANES
            y_ref[:, c0:c0 + LANES] = y_scr[half, pl.ds(b, lc, stride=SUBLANES), :]

    @pl.when(n == pl.num_programs(1) - 1)
    def _():
        fin_ref[...] = st_scr[...]


def _ssm(u_t, bmat, cmat, a, s0, *, batch, seq, lc, reverse):
    nch = seq // lc
    chunk = (lambda n: nch - 1 - n) if reverse else (lambda n: n)
    const = lambda b, n: (0, 0)
    return pl.pallas_call(
        functools.partial(_ssm_kernel, lc=lc, n_sub=4, reverse=reverse),
        grid=(batch // SUBLANES, nch),
        in_specs=[
            pl.BlockSpec((lc, SUBLANES, W_GRP), lambda b, n: (chunk(n), b, 0)),
            pl.BlockSpec((W_GRP, 2 * N_STATE), const),
            pl.BlockSpec((2 * N_STATE, W_GRP), const),
            pl.BlockSpec((2, N_STATE), const),
            pl.BlockSpec((SUBLANES, 2 * N_STATE), lambda b, n: (b, 0)),
        ],
        out_specs=[
            pl.BlockSpec((lc, SUBLANES * W_GRP), lambda b, n: (chunk(n), b)),
            pl.BlockSpec((SUBLANES, 2 * N_STATE), lambda b, n: (b, 0)),
        ],
        out_shape=[
            jax.ShapeDtypeStruct((seq, batch * W_GRP), F32),
            jax.ShapeDtypeStruct((batch, 2 * N_STATE), F32),
        ],
        scratch_shapes=[
            pltpu.VMEM((lc * SUBLANES, 2 * N_STATE), F32),
            pltpu.VMEM((W_GRP // LANES, lc * SUBLANES, LANES), F32),
            pltpu.VMEM((SUBLANES, 2 * N_STATE), F32),
        ],
        compiler_params=_cparams(("parallel", "arbitrary")),
        name="ssm_scan",
    )(u_t, bmat, cmat, a, s0)


def _dft_kernel(t1c_ref, t1s_ref, t2c_ref, t2s_ref, pc_ref, ps_ref, o_ref, lc_scr, ls_scr, acc_scr,
                *, tm, scale):
    kb = pl.program_id(2)
    t2c = t2c_ref[...]
    t2s = t2s_ref[...]
    for a in range(tm // DFT_RADIX):
        c1 = t1c_ref[a:a + 1, :]
        s1 = t1s_ref[a:a + 1, :]
        rows = slice(a * DFT_RADIX, (a + 1) * DFT_RADIX)
        lc_scr[rows, :] = (c1 * t2c - s1 * t2s).astype(BF16)
        ls_scr[rows, :] = (s1 * t2c + c1 * t2s).astype(BF16)
    part = (jnp.dot(lc_scr[...], pc_ref[...], preferred_element_type=F32)
            - jnp.dot(ls_scr[...], ps_ref[...], preferred_element_type=F32))

    @pl.when(kb == 0)
    def _():
        acc_scr[...] = part

    @pl.when(kb > 0)
    def _():
        acc_scr[...] += part

    @pl.when(kb == pl.num_programs(2) - 1)
    def _():
        o_ref[...] = (acc_scr[...] * scale).astype(BF16)


def _dft(tabs, pc, ps, *, seq, tm, tk, tn):
    nc = pc.shape[1]
    t1c, t1s, t2c, t2s = tabs
    ra = tm // DFT_RADIX
    scale = 1.0 / math.sqrt(seq * W_GRP)
    return pl.pallas_call(
        functools.partial(_dft_kernel, tm=tm, scale=scale),
        grid=(seq // tm, nc // tn, seq // tk),
        in_specs=[
            pl.BlockSpec((ra, tk), lambda i, j, k: (i, k)),
            pl.BlockSpec((ra, tk), lambda i, j, k: (i, k)),
            pl.BlockSpec((DFT_RADIX, tk), lambda i, j, k: (0, k)),
            pl.BlockSpec((DFT_RADIX, tk), lambda i, j, k: (0, k)),
            pl.BlockSpec((tk, tn), lambda i, j, k: (k, j)),
            pl.BlockSpec((tk, tn), lambda i, j, k: (k, j)),
        ],
        out_specs=pl.BlockSpec((tm, tn), lambda i, j, k: (i, j)),
        out_shape=jax.ShapeDtypeStruct((seq, nc), BF16),
        scratch_shapes=[pltpu.VMEM((tm, tk), BF16), pltpu.VMEM((tm, tk), BF16), pltpu.VMEM((tm, tn), F32)],
        compiler_params=_cparams(("parallel", "parallel", "arbitrary")),
        name="pos_dft",
    )(t1c, t1s, t2c, t2s, pc, ps)


def _mix_out_kernel(h_ref, mod_ref, yf_ref, yb_ref, u_ref, att_ref, f_ref, d_ref, wglu_ref, bglu_ref,
                    wf_ref, bf_ref, wo_ref, o_ref, m_scr):
    y = d_ref[...] * u_ref[...] + yf_ref[...] + yb_ref[...]
    y = y * (0.5 * (1.0 + jnp.tanh(math.sqrt(2.0 / math.pi) * (y + 0.044715 * (y * y * y)))))
    z = jnp.dot(y.astype(BF16), wglu_ref[...], preferred_element_type=F32) + bglu_ref[...]
    m_scr[:, 0:256] = (y * jax.nn.sigmoid(z)).astype(BF16)
    m_scr[:, 256:768] = att_ref[...]
    m_scr[:, 768:1024] = (jnp.dot(f_ref[...], wf_ref[...], preferred_element_type=F32) + bf_ref[...]).astype(BF16)
    mixed = jnp.dot(m_scr[...], wo_ref[...], preferred_element_type=F32)
    o_ref[...] = h_ref[...] + mod_ref[0, 5:6, :] * mixed


def _mix_out(h, mod, y_fwd, y_bwd, u_t, att, f, d, wglu, bglu, wf, bfn, wo, *, batch, seq, tm, cond_base,
             per_batch_cond):
    t = batch * seq
    tiles = seq // tm
    cond = lambda i: (cond_base + (i // tiles if per_batch_cond else 0), 0, 0)
    tcol = lambda i: (i % tiles, i // tiles)
    const = lambda i: (0, 0)
    return pl.pallas_call(
        _mix_out_kernel,
        grid=(t // tm,),
        in_specs=[
            pl.BlockSpec((tm, D_MODEL), lambda i: (i, 0)),
            pl.BlockSpec((1, N_MOD, D_MODEL), cond),
            pl.BlockSpec((tm, W_GRP), tcol),
            pl.BlockSpec((tm, W_GRP), tcol),
            pl.BlockSpec((tm, W_GRP), tcol),
            pl.BlockSpec((tm, 512), lambda i: (i, 0)),
            pl.BlockSpec((tm, W_GRP), tcol),
            pl.BlockSpec((1, W_GRP), const),
            pl.BlockSpec((W_GRP, W_GRP), const),
            pl.BlockSpec((1, W_GRP), const),
            pl.BlockSpec((W_GRP, W_GRP), const),
            pl.BlockSpec((1, W_GRP), const),
            pl.BlockSpec((D_MODEL, D_MODEL), const),
        ],
        out_specs=pl.BlockSpec((tm, D_MODEL), lambda i: (i, 0)),
        out_shape=jax.ShapeDtypeStruct((t, D_MODEL), F32),
        scratch_shapes=[pltpu.VMEM((tm, D_MODEL), BF16)],
        compiler_params=_cparams(("parallel",)),
        name="mix_out",
    )(h, mod, y_fwd, y_bwd, u_t, att, f, d, wglu, bglu, wf, bfn, wo)


def _rope_tables(seq):
    rows = seq // GRID_W
    row_id = jnp.repeat(jnp.arange(rows), GRID_W).astype(F32)
    col_id = jnp.tile(jnp.arange(GRID_W), rows).astype(F32)
    n_freq = HEAD_DIM // 4
    inv = ROPE_BASE ** (-jnp.arange(n_freq, dtype=F32) / n_freq)
    ang = jnp.concatenate([row_id[:, None] * inv, col_id[:, None] * inv], axis=-1)
    cos, sin = jnp.cos(ang), jnp.sin(ang)
    cos_full = jnp.tile(jnp.concatenate([cos, cos], axis=-1), (1, N_Q_HEADS))
    sin_signed = jnp.tile(jnp.concatenate([-sin, sin], axis=-1), (1, N_Q_HEADS))
    return cos_full, sin_signed


def _angle_table(mult, n, period):
    m = (mult[:, None] * n[None, :]) % period
    th = m.astype(F32) * (2.0 * math.pi / period)
    return jnp.cos(th), jnp.sin(th)


def _dft_tables(seq):
    n = jnp.arange(seq, dtype=jnp.int32)
    t1c, t1s = _angle_table(DFT_RADIX * jnp.arange(seq // DFT_RADIX, dtype=jnp.int32), n, seq)
    t2c, t2s = _angle_table(jnp.arange(DFT_RADIX, dtype=jnp.int32), n, seq)
    return t1c, t1s, t2c, t2s


def _ssm_params(lam_re, lam_im, b_re, b_im, c_re, c_im, log_dt):
    dt = jnp.exp(log_dt)[..., None]
    mag = jnp.exp(lam_re * dt)
    ar = mag * jnp.cos(lam_im * dt)
    ai = mag * jnp.sin(lam_im * dt)
    den = lam_re * lam_re + lam_im * lam_im
    qr = ((ar - 1.0) * lam_re + ai * lam_im) / den
    qi = (ai * lam_re - (ar - 1.0) * lam_im) / den
    bb_re = qr[..., None] * b_re - qi[..., None] * b_im
    bb_im = qr[..., None] * b_im + qi[..., None] * b_re
    eye = jnp.eye(N_SSM_GROUPS, dtype=F32)

    def block_diag(x):
        y = jnp.transpose(x, (0, 1, 3, 2))[:, :, :, None, :] * eye[None, :, None, :, None]
        return y.reshape(2, x.shape[1] * x.shape[3], x.shape[1] * x.shape[2])

    bmat = jnp.concatenate([block_diag(bb_re), block_diag(bb_im)], axis=-1).astype(BF16)
    cmat = jnp.concatenate([block_diag(c_re), block_diag(-c_im)], axis=1).astype(BF16)
    a2 = jnp.stack([ar.reshape(2, N_STATE), ai.reshape(2, N_STATE)], axis=1)
    return a2, bmat, cmat


def _permute_q_heads(w_in):
    cols = jnp.arange(P_IN, dtype=jnp.int32)
    for base in (256, 768):
        blk = jnp.concatenate([base + h * HEAD_DIM + jnp.arange(HEAD_DIM, dtype=jnp.int32) for h in Q_HEAD_ORDER])
        cols = cols.at[base:base + N_Q_HEADS * HEAD_DIM].set(blk)
    return jnp.take(w_in, cols, axis=1)


def _cache_layout(cache):
    b, _, past, nkv, dh = cache.shape
    keys = cache[:, 0].reshape(b, past, nkv * dh).astype(BF16)
    vt = jnp.transpose(cache[:, 1], (0, 2, 3, 1))
    ones = jnp.ones((b, nkv, VT_ROWS - dh, past), vt.dtype)
    vt = jnp.concatenate([vt, ones], axis=2).reshape(b, nkv * VT_ROWS, past).astype(BF16)
    return keys, vt


def kernel(x_prompt, x_sample, cache_swa_kv, cache_axial_kv, state_ssm, c, c_ctx, w_mod, b_mod, norm_ffn1, norm_mix, norm_ffn2, ffn1_w_gate, ffn1_w_up, ffn1_w_down, ffn2_w_gate, ffn2_w_up, ffn2_w_down, w_in, w_out, ssm_lambda_re, ssm_lambda_im, ssm_b_re, ssm_b_im, ssm_c_re, ssm_c_im, ssm_log_dt, ssm_d, ssm_w_glu, ssm_b_glu, swa_sink, ax_q_norm, ax_k_norm, fnet_w, fnet_b, final_norm):
    depth = w_mod.shape[0]
    cb, cl, _ = x_prompt.shape
    lb, ll, _ = x_sample.shape

    cond = jnp.zeros((16, D_MODEL), F32).at[0].set(c_ctx).at[1:1 + lb].set(c)
    mod_all = _modulation(cond, w_mod, b_mod.reshape(depth, 1, N_MOD * D_MODEL))
    mod_all = mod_all.reshape(depth, 16, N_MOD, D_MODEL)

    e_heads = jnp.kron(jnp.eye(256 // HEAD_DIM, dtype=F32), jnp.ones((HEAD_DIM, HEAD_DIM), F32)) / HEAD_DIM
    e_heads = e_heads.astype(BF16)
    kc = jnp.arange(W_GRP, dtype=jnp.int32)
    cc, sc = _angle_table(kc, kc, W_GRP)
    cs_chan = jnp.concatenate([cc, sc], axis=-1).astype(BF16)
    rope_tabs = _rope_tables(ll)
    dft_ctx = _dft_tables(cl)
    dft_lat = _dft_tables(ll)
    fg = final_norm.reshape(1, D_MODEL)

    h_ctx = x_prompt.reshape(cb * cl, D_MODEL)
    h_lat = x_sample.reshape(lb * ll, D_MODEL)
    tm_lat = 512
    swa_list, ax_list, ssm_list = [], [], []
    for l in range(depth):
        mod = mod_all[l]
        bf = lambda w: w[l].astype(BF16)
        row = lambda v: v[l].reshape(1, -1)
        ffn1 = (row(norm_ffn1), bf(ffn1_w_gate), bf(ffn1_w_up), bf(ffn1_w_down), fg)
        ffn2 = (row(norm_ffn2), bf(ffn2_w_gate), bf(ffn2_w_up), bf(ffn2_w_down), fg)
        w_in_l, w_out_l = _permute_q_heads(w_in[l]).astype(BF16), bf(w_out)
        qn = jnp.tile(ax_q_norm[l], N_Q_HEADS).reshape(1, 256)
        kn = jnp.tile(ax_k_norm[l], KV_W // HEAD_DIM).reshape(1, KV_W)
        a2, bmat, cmat = _ssm_params(ssm_lambda_re[l], ssm_lambda_im[l], ssm_b_re[l], ssm_b_im[l],
                                     ssm_c_re[l], ssm_c_im[l], ssm_log_dt[l])
        mix_out_w = (row(ssm_d), bf(ssm_w_glu), row(ssm_b_glu), bf(fnet_w), row(fnet_b), w_out_l)
        sink = swa_sink[l]
        last = l == depth - 1

        h_ctx = _ffn(h_ctx, mod, *ffn1, mod_base=0, tm=512, tiles_per_cond=1 << 30, cond_base=0, final_norm=False)
        u_t, qkv, pc, ps, kv_s, kv_a = _mix_in(h_ctx, mod, row(norm_mix), w_in_l, qn, kn, e_heads, cs_chan, None,
                                               batch=cb, seq=cl, tm=cl, cond_base=0, rope=False)
        u3 = u_t.reshape(cl, cb, W_GRP)
        zero_state = jnp.zeros((cb, 2 * N_STATE), F32)
        y_dirs, s_fins = zip(*[_ssm(u3, bmat[d], cmat[d], a2[d], zero_state, batch=cb, seq=cl, lc=128,
                                    reverse=bool(d)) for d in range(2)])
        att = _attn_ctx(sink, qkv, batch=cb, seq=cl)
        f = _dft(dft_ctx, pc, ps, seq=cl, tm=cl, tk=cl, tn=2048)
        h_ctx = _mix_out(h_ctx, mod, *y_dirs, u_t, att, f, *mix_out_w,
                         batch=cb, seq=cl, tm=cl, cond_base=0, per_batch_cond=False)
        h_ctx = _ffn(h_ctx, mod, *ffn2, mod_base=6, tm=512, tiles_per_cond=1 << 30, cond_base=0, final_norm=last)
        swa_list.append(kv_s.reshape(cb, 2, cl, KV_W // HEAD_DIM, HEAD_DIM))
        ax_list.append(kv_a.reshape(cb, 2, cl, KV_W // HEAD_DIM, HEAD_DIM))
        s_fin = jnp.stack(s_fins, axis=0).reshape(2, cb, 2, N_SSM_GROUPS, SSM_STATE)
        ssm_list.append(jnp.transpose(s_fin, (1, 0, 3, 4, 2)))

        tpc = ll // tm_lat
        h_lat = _ffn(h_lat, mod, *ffn1, mod_base=0, tm=tm_lat, tiles_per_cond=tpc, cond_base=1, final_norm=False)
        u_t, qkv, pc, ps, vt = _mix_in(h_lat, mod, row(norm_mix), w_in_l, qn, kn, e_heads, cs_chan, rope_tabs,
                                       batch=lb, seq=ll, tm=tm_lat, cond_base=1, rope=True)
        s0 = jnp.transpose(state_ssm[:, l], (1, 0, 4, 2, 3)).reshape(2, lb, 2 * N_STATE)
        u3 = u_t.reshape(ll, lb, W_GRP)
        y_dirs = [_ssm(u3, bmat[d], cmat[d], a2[d], s0[d], batch=lb, seq=ll, lc=128, reverse=bool(d))[0]
                  for d in range(2)]
        kc_s, vc_s = _cache_layout(cache_swa_kv[:, l])
        kc_a, vc_a = _cache_layout(cache_axial_kv[:, l])
        att = _attn_lat(sink, qkv, vt, kc_s, vc_s, kc_a, vc_a, batch=lb, seq=ll, tq=256)
        f = _dft(dft_lat, pc, ps, seq=ll, tm=512, tk=1024, tn=2048)
        h_lat = _mix_out(h_lat, mod, *y_dirs, u_t, att, f, *mix_out_w,
                         batch=lb, seq=ll, tm=tm_lat, cond_base=1, per_batch_cond=True)
        h_lat = _ffn(h_lat, mod, *ffn2, mod_base=6, tm=tm_lat, tiles_per_cond=tpc, cond_base=1, final_norm=last)

    y_prompt = h_ctx.reshape(cb, cl, D_MODEL)
    y_sample = h_lat.reshape(lb, ll, D_MODEL)
    return (y_prompt, y_sample, jnp.stack(swa_list, axis=1), jnp.stack(ax_list, axis=1),
            jnp.stack(ssm_list, axis=1))
```

```python
import functools
import math

import jax
import jax.numpy as jnp
from jax import lax
from jax.experimental import pallas as pl
from jax.experimental.pallas import tpu as pltpu

F32 = jnp.float32
BF16 = jnp.bfloat16

D_MODEL = 1024
D_FF = 2816
N_MOD = 9
EPS = 1e-6
HEAD_DIM = 64
HALF_HEAD = HEAD_DIM // 2
N_Q_HEADS = 4
Q_PER_KV = 2
W_GRP = 256
KV_W = 128
P_IN = 1536
N_SSM_GROUPS = 16
SSM_GROUP = 16
SSM_STATE = 64
N_STATE = N_SSM_GROUPS * SSM_STATE
WINDOW = 128
GRID_W = 64
ROPE_BASE = 10000.0
NEG_BIG = -1e30
LOG2E = math.log2(math.e)
Q_HEAD_ORDER = (0, 2, 1, 3)
N_KV_HEADS = N_Q_HEADS // Q_PER_KV
VT_ROWS = 80

FF_CHUNKS = (1024, 1024, 768)
SUBLANES = 8
LANES = 128
DFT_RADIX = 64
VMEM_LIMIT = 56 * 1024 * 1024


def _cparams(sem):
    return pltpu.CompilerParams(dimension_semantics=sem, vmem_limit_bytes=VMEM_LIMIT)


def _adaln(h, g, sc, sh):
    ms = jnp.mean(h * h, axis=-1, keepdims=True)
    return (h * lax.rsqrt(ms + EPS) * g) * (1.0 + sc) + sh


def _mod_kernel(c_ref, w_ref, b_ref, o_ref):
    c = c_ref[...]
    a = (c * jax.nn.sigmoid(c)).astype(BF16)
    o_ref[...] = jnp.dot(a, w_ref[...].astype(BF16), preferred_element_type=F32) + b_ref[...]


def _modulation(cond, w_mod, b_mod):
    depth = w_mod.shape[0]
    n = N_MOD * D_MODEL
    tn = 1024
    return pl.pallas_call(
        _mod_kernel,
        grid=(depth, n // tn),
        in_specs=[
            pl.BlockSpec((16, D_MODEL), lambda l, j: (0, 0)),
            pl.BlockSpec((None, D_MODEL, tn), lambda l, j: (l, 0, j)),
            pl.BlockSpec((None, 1, tn), lambda l, j: (l, 0, j)),
        ],
        out_specs=pl.BlockSpec((None, 16, tn), lambda l, j: (l, 0, j)),
        out_shape=jax.ShapeDtypeStruct((depth, 16, n), F32),
        compiler_params=_cparams(("parallel", "parallel")),
        name="modulation",
    )(cond, w_mod, b_mod)


def _ffn_kernel(h_ref, mod_ref, g_ref, wg_ref, wu_ref, wd_ref, fg_ref, o_ref, *, mod_base, final_norm):
    h = h_ref[...]
    x = _adaln(h, g_ref[...], mod_ref[0, mod_base + 1:mod_base + 2, :],
               mod_ref[0, mod_base:mod_base + 1, :]).astype(BF16)
    acc = None
    c0 = 0
    for width in FF_CHUNKS:
        gate = jnp.dot(x, wg_ref[:, c0:c0 + width], preferred_element_type=F32)
        up = jnp.dot(x, wu_ref[:, c0:c0 + width], preferred_element_type=F32)
        hg = 0.5 * gate
        a = (hg * (1.0 + jnp.tanh(hg)) * up).astype(BF16)
        part = jnp.dot(a, wd_ref[c0:c0 + width, :], preferred_element_type=F32)
        acc = part if acc is None else acc + part
        c0 += width
    hn = h + (0.5 * mod_ref[0, mod_base + 2:mod_base + 3, :]) * acc
    if final_norm:
        ms = jnp.mean(hn * hn, axis=-1, keepdims=True)
        hn = hn * lax.rsqrt(ms + EPS) * fg_ref[...]
    o_ref[...] = hn


def _ffn(h, mod, g, wg, wu, wd, fg, *, layer, mod_base, tm, tiles_per_cond, cond_base, final_norm):
    t = h.shape[0]
    cond = lambda i: (cond_base + i // tiles_per_cond, 0, 0)
    const = lambda i: (0, 0)
    of_layer = lambda i: (layer, 0, 0)
    resident = dict(pipeline_mode=pl.Buffered(1))
    kern = functools.partial(_ffn_kernel, mod_base=mod_base, final_norm=final_norm)
    return pl.pallas_call(
        kern,
        grid=(t // tm,),
        in_specs=[
            pl.BlockSpec((tm, D_MODEL), lambda i: (i, 0)),
            pl.BlockSpec((1, N_MOD, D_MODEL), cond),
            pl.BlockSpec((1, D_MODEL), const),
            pl.BlockSpec((None, D_MODEL, D_FF), of_layer, **resident),
            pl.BlockSpec((None, D_MODEL, D_FF), of_layer, **resident),
            pl.BlockSpec((None, D_FF, D_MODEL), of_layer, **resident),
            pl.BlockSpec((1, D_MODEL), const),
        ],
        out_specs=pl.BlockSpec((tm, D_MODEL), lambda i: (i, 0)),
        out_shape=jax.ShapeDtypeStruct((t, D_MODEL), F32),
        compiler_params=_cparams(("parallel",)),
        name="ffn",
    )(h, mod, g, wg, wu, wd, fg)


def _head_mean_sq(x, e):
    x2 = x * x
    hi = x2.astype(BF16)
    lo = (x2 - hi.astype(F32)).astype(BF16)
    return jnp.dot(hi, e, preferred_element_type=F32) + jnp.dot(lo, e, preferred_element_type=F32)


def _swap_halves(x):
    w = x.shape[-1]
    lane = lax.broadcasted_iota(jnp.int32, x.shape, 1)
    first = (lane & HALF_HEAD) == 0
    return jnp.where(first, pltpu.roll(x, w - HALF_HEAD, 1), pltpu.roll(x, HALF_HEAD, 1))


def _mix_in_kernel(*refs, rope):
    if rope:
        (h_ref, mod_ref, g_ref, w_ref, qn_ref, kn_ref, e_ref, cs_ref, cos_ref, sin_ref,
         u_ref, qkv_ref, pc_ref, ps_ref, vt_ref) = refs
    else:
        (h_ref, mod_ref, g_ref, w_ref, qn_ref, kn_ref, e_ref, cs_ref,
         u_ref, qkv_ref, pc_ref, ps_ref, kvs_ref, kva_ref) = refs
    x = _adaln(h_ref[...], g_ref[...], mod_ref[0, 4:5, :], mod_ref[0, 3:4, :]).astype(BF16)
    proj = jnp.dot(x, w_ref[...], preferred_element_type=F32)
    u_ssm = proj[:, 0:256]
    q_s = proj[:, 256:512]
    k_s = proj[:, 512:640]
    v_s = proj[:, 640:768]
    q_a = proj[:, 768:1024]
    k_a = proj[:, 1024:1152]
    v_a = proj[:, 1152:1280]
    u_f = proj[:, 1280:1536]

    e = e_ref[...]
    q_a = q_a * lax.rsqrt(_head_mean_sq(q_a, e) + EPS) * qn_ref[...]
    k_a = k_a * lax.rsqrt(_head_mean_sq(k_a, e[0:KV_W, 0:KV_W]) + EPS) * kn_ref[...]

    u_ref[...] = u_ssm
    p = jnp.dot(u_f.astype(BF16), cs_ref[...], preferred_element_type=F32)
    pc_ref[...] = p[:, 0:256].astype(BF16)
    ps_ref[...] = p[:, 256:512].astype(BF16)

    if rope:
        scale = LOG2E * HEAD_DIM ** -0.5
        cos = cos_ref[...]
        sin = sin_ref[...]
        rot = lambda t, w: t * cos[:, 0:w] + _swap_halves(t) * sin[:, 0:w]
        qkv_ref[:, 0:256] = (q_s * scale).astype(BF16)
        qkv_ref[:, 256:512] = (rot(q_s, 256) * scale).astype(BF16)
        qkv_ref[:, 512:768] = (q_a * scale).astype(BF16)
        qkv_ref[:, 768:1024] = (rot(q_a, 256) * scale).astype(BF16)
        qkv_ref[:, 1024:1152] = rot(k_s, KV_W).astype(BF16)
        qkv_ref[:, 1152:1280] = rot(k_a, KV_W).astype(BF16)
        ones = jnp.ones((VT_ROWS - HEAD_DIM, v_s.shape[0]), BF16)
        for grp, v in enumerate((v_s, v_a)):
            vt = jnp.transpose(v).astype(BF16)
            for kv in range(N_KV_HEADS):
                r0 = (grp * N_KV_HEADS + kv) * VT_ROWS
                vt_ref[r0:r0 + HEAD_DIM, :] = vt[kv * HEAD_DIM:(kv + 1) * HEAD_DIM, :]
                vt_ref[r0 + HEAD_DIM:r0 + VT_ROWS, :] = ones
    else:
        scale = HEAD_DIM ** -0.5
        qkv_ref[:, 0:256] = (q_s * scale).astype(BF16)
        qkv_ref[:, 256:384] = k_s.astype(BF16)
        qkv_ref[:, 384:512] = v_s.astype(BF16)
        qkv_ref[:, 512:768] = (q_a * scale).astype(BF16)
        qkv_ref[:, 768:896] = k_a.astype(BF16)
        qkv_ref[:, 896:1024] = v_a.astype(BF16)
        kvs_ref[0] = k_s
        kvs_ref[1] = v_s
        kva_ref[0] = k_a
        kva_ref[1] = v_a


def _mix_in(h, mod, g, w_in, qn, kn, e, cs, rope_tabs, *, batch, seq, tm, cond_base, rope):
    t = batch * seq
    tiles = seq // tm
    nc = batch * W_GRP
    cond = lambda i: (cond_base + (i // tiles if rope else 0), 0, 0)
    tcol = lambda i: (i % tiles, i // tiles)
    const = lambda i: (0, 0)
    in_specs = [
        pl.BlockSpec((tm, D_MODEL), lambda i: (i, 0)),
        pl.BlockSpec((1, N_MOD, D_MODEL), cond),
        pl.BlockSpec((1, D_MODEL), const),
        pl.BlockSpec((D_MODEL, P_IN), const),
        pl.BlockSpec((1, 256), const),
        pl.BlockSpec((1, KV_W), const),
        pl.BlockSpec((256, 256), const),
        pl.BlockSpec((256, 512), const),
    ]
    args = [h, mod, g, w_in, qn, kn, e, cs]
    qkv_w = 1280 if rope else 1024
    out_specs = [
        pl.BlockSpec((tm, W_GRP), tcol),
        pl.BlockSpec((tm, qkv_w), lambda i: (i, 0)),
        pl.BlockSpec((tm, W_GRP), tcol),
        pl.BlockSpec((tm, W_GRP), tcol),
    ]
    out_shape = [
        jax.ShapeDtypeStruct((seq, nc), F32),
        jax.ShapeDtypeStruct((t, qkv_w), BF16),
        jax.ShapeDtypeStruct((seq, nc), BF16),
        jax.ShapeDtypeStruct((seq, nc), BF16),
    ]
    if rope:
        in_specs += [pl.BlockSpec((tm, 256), lambda i: (i % tiles, 0))] * 2
        args += list(rope_tabs)
        out_specs.append(pl.BlockSpec((None, 2 * N_KV_HEADS * VT_ROWS, tm), lambda i: (i // tiles, 0, i % tiles)))
        out_shape.append(jax.ShapeDtypeStruct((batch, 2 * N_KV_HEADS * VT_ROWS, seq), BF16))
    else:
        assert tm == seq
        out_specs += [pl.BlockSpec((None, 2, seq, KV_W), lambda i: (i, 0, 0, 0))] * 2
        out_shape += [jax.ShapeDtypeStruct((batch, 2, seq, KV_W), F32)] * 2
    return pl.pallas_call(
        functools.partial(_mix_in_kernel, rope=rope),
        grid=(t // tm,),
        in_specs=in_specs,
        out_specs=out_specs,
        out_shape=out_shape,
        compiler_params=_cparams(("parallel",)),
        name="mix_in",
    )(*args)


def _qk(a, b):
    return lax.dot_general(a, b, (((1,), (1,)), ((), ())), preferred_element_type=F32)


def _attn_ctx_kernel(sink_ref, qkv_ref, o_ref):
    for grp in range(2):
        base = grp * 512
        for pos, h in enumerate(Q_HEAD_ORDER):
            kv = h // Q_PER_KV
            q = qkv_ref[:, base + pos * HEAD_DIM:base + (pos + 1) * HEAD_DIM]
            k = qkv_ref[:, base + 256 + kv * HEAD_DIM:base + 256 + (kv + 1) * HEAD_DIM]
            v = qkv_ref[:, base + 384 + kv * HEAD_DIM:base + 384 + (kv + 1) * HEAD_DIM]
            s = _qk(q, k)
            m = jnp.max(s, axis=-1, keepdims=True)
            if grp == 0:
                m = jnp.maximum(m, sink_ref[h])
            p = jnp.exp(s - m)
            l = jnp.sum(p, axis=-1, keepdims=True)
            if grp == 0:
                l = l + jnp.exp(sink_ref[h] - m)
            o = jnp.dot(p.astype(BF16), v, preferred_element_type=F32) / l
            o_ref[:, grp * 256 + h * HEAD_DIM:grp * 256 + (h + 1) * HEAD_DIM] = o.astype(BF16)


def _attn_ctx(sink, qkv, *, batch, seq):
    return pl.pallas_call(
        _attn_ctx_kernel,
        grid=(batch,),
        in_specs=[
            pl.BlockSpec(memory_space=pltpu.SMEM),
            pl.BlockSpec((seq, 1024), lambda b: (b, 0)),
        ],
        out_specs=pl.BlockSpec((seq, 512), lambda b: (b, 0)),
        out_shape=jax.ShapeDtypeStruct((batch * seq, 512), BF16),
        compiler_params=_cparams(("parallel",)),
        name="attn_ctx",
    )(sink, qkv)


def _attn_lat_kernel(sink_ref, zero_ref, q_ref, k_ref, vt_ref, kcs_ref, vcs_ref, kca_ref, vca_ref, o_ref,
                     qm_scr, m_scr, acc_scr, s0_scr, s1_scr, bm_scr, *, tq, seq, kc):
    i = pl.program_id(1)
    q0 = i * tq
    win = tq + 2 * WINDOW
    past = kcs_ref.shape[0]
    assert win <= kc and past <= kc
    ws = pl.multiple_of(jnp.clip(q0 - WINDOW, 0, seq - win), WINDOW)
    lane = lax.broadcasted_iota(jnp.int32, (Q_PER_KV * tq, KV_W), 1)

    def banded(s):
        kpos = ws + lax.broadcasted_iota(jnp.int32, s.shape, 0)
        qpos = q0 + (lax.broadcasted_iota(jnp.int32, s.shape, 1) & (tq - 1))
        return jnp.where(jnp.abs(qpos - kpos) <= WINDOW, s, NEG_BIG)

    def sink_row(kv):
        return jnp.concatenate(
            [jnp.full((1, tq), sink_ref[Q_PER_KV * kv + j] * LOG2E, F32) for j in range(Q_PER_KV)], axis=1)

    def vt_rows(grp, kv):
        r0 = (grp * N_KV_HEADS + kv) * VT_ROWS
        return slice(r0, r0 + VT_ROWS)

    acc_scr[...] = jnp.zeros(acc_scr.shape, F32)
    for grp in range(2):
        base = grp * 512
        q_plain = jnp.concatenate([q_ref[:, base:base + KV_W], q_ref[:, base + KV_W:base + 2 * KV_W]], axis=0)
        q_rope = jnp.concatenate([q_ref[:, base + 256:base + 256 + KV_W],
                                  q_ref[:, base + 256 + KV_W:base + 256 + 2 * KV_W]], axis=0)
        for kv in range(N_KV_HEADS):
            keep = (lane < HEAD_DIM) if kv == 0 else (lane >= HEAD_DIM)
            qm_scr[grp, 0, kv] = jnp.where(keep, q_plain, 0)
            qm_scr[grp, 1, kv] = jnp.where(keep, q_rope, 0)
            m_scr[grp, kv] = sink_row(kv) if grp == 0 else jnp.full((1, Q_PER_KV * tq), NEG_BIG, F32)

    jobs = [
        (0, past, lambda kv: _qk(kcs_ref[...], qm_scr[0, 0, kv]),
         lambda kv: vcs_ref[kv * VT_ROWS:(kv + 1) * VT_ROWS, :]),
        (0, win, lambda kv: banded(_qk(k_ref[pl.ds(ws, win), 0:KV_W], qm_scr[0, 1, kv])),
         lambda kv: vt_ref[vt_rows(0, kv), pl.ds(ws, win)]),
        (1, past, lambda kv: _qk(kca_ref[...], qm_scr[1, 0, kv]),
         lambda kv: vca_ref[kv * VT_ROWS:(kv + 1) * VT_ROWS, :]),
    ]
    for c in range(seq // kc):
        jobs.append((1, kc, lambda kv, c=c: _qk(k_ref[c * kc:(c + 1) * kc, KV_W:2 * KV_W], qm_scr[1, 1, kv]),
                     lambda kv, c=c: vt_ref[vt_rows(1, kv), c * kc:(c + 1) * kc]))

    s_slots = (s0_scr, s1_scr)
    row0 = pl.multiple_of(zero_ref[0], SUBLANES)

    def scores_to(slot, job):
        _, n, scores, _ = job
        for kv in range(N_KV_HEADS):
            s = scores(kv)
            s_slots[slot][kv, 0:n, :] = s
            bm_scr[slot, kv] = jnp.max(s, axis=0, keepdims=True)

    def update_from(slot, job):
        grp, n, _, values = job
        for kv in range(N_KV_HEADS):
            m_old = m_scr[grp, kv]
            m_new = jnp.maximum(m_old, bm_scr[slot, kv])
            p = jnp.exp2((s_slots[slot][kv, pl.ds(row0, n), :] - m_new).astype(BF16))
            acc_scr[grp, kv] = (jnp.exp2(m_old - m_new) * acc_scr[grp, kv]
                                + jnp.dot(values(kv), p, preferred_element_type=F32))
            m_scr[grp, kv] = m_new

    def finalize(grp):
        outs = []
        for kv in range(N_KV_HEADS):
            acc = acc_scr[grp, kv]
            den = acc[HEAD_DIM:HEAD_DIM + 1, :]
            if grp == 0:
                den = den + jnp.exp2(sink_row(kv) - m_scr[grp, kv])
            o = acc[0:HEAD_DIM, :] / den
            outs += [o[:, j * tq:(j + 1) * tq] for j in range(Q_PER_KV)]
        o_t = jnp.concatenate(outs, axis=0)
        o_ref[:, grp * 256:(grp + 1) * 256] = jnp.transpose(o_t).astype(BF16)

    scores_to(0, jobs[0])
    for j, job in enumerate(jobs):
        if j + 1 < len(jobs):
            scores_to((j + 1) % 2, jobs[j + 1])
        update_from(j % 2, job)
        if j + 1 == len(jobs) or jobs[j + 1][0] != job[0]:
            finalize(job[0])


def _attn_lat(sink, qkv, vt, kc_s, vc_s, kc_a, vc_a, *, batch, seq, tq, kc=512):
    nq = seq // tq
    past = kc_s.shape[1]
    per_b3 = lambda b, i: (b, 0, 0)
    return pl.pallas_call(
        functools.partial(_attn_lat_kernel, tq=tq, seq=seq, kc=kc),
        grid=(batch, nq),
        in_specs=[
            pl.BlockSpec(memory_space=pltpu.SMEM),
            pl.BlockSpec(memory_space=pltpu.SMEM),
            pl.BlockSpec((tq, 1024), lambda b, i: (b * nq + i, 0)),
            pl.BlockSpec((seq, 256), lambda b, i: (b, 4)),
            pl.BlockSpec((None, 2 * N_KV_HEADS * VT_ROWS, seq), per_b3),
            pl.BlockSpec((None, past, KV_W), per_b3),
            pl.BlockSpec((None, N_KV_HEADS * VT_ROWS, past), per_b3),
            pl.BlockSpec((None, past, KV_W), per_b3),
            pl.BlockSpec((None, N_KV_HEADS * VT_ROWS, past), per_b3),
        ],
        out_specs=pl.BlockSpec((tq, 512), lambda b, i: (b * nq + i, 0)),
        out_shape=jax.ShapeDtypeStruct((batch * seq, 512), BF16),
        scratch_shapes=[
            pltpu.VMEM((2, 2, N_KV_HEADS, Q_PER_KV * tq, KV_W), BF16),
            pltpu.VMEM((2, N_KV_HEADS, 1, Q_PER_KV * tq), F32),
            pltpu.VMEM((2, N_KV_HEADS, VT_ROWS, Q_PER_KV * tq), F32),
            pltpu.VMEM((N_KV_HEADS, kc, Q_PER_KV * tq), F32),
            pltpu.VMEM((N_KV_HEADS, kc, Q_PER_KV * tq), F32),
            pltpu.VMEM((2, N_KV_HEADS, 1, Q_PER_KV * tq), F32),
        ],
        compiler_params=_cparams(("parallel", "parallel")),
        name="attn_lat",
    )(sink, jnp.zeros((1,), jnp.int32), qkv, qkv, vt, kc_s, vc_s, kc_a, vc_a)


def _ssm_kernel(u_ref, bm_ref, cm_ref, a_ref, s0_ref, y_ref, fin_ref, bu_scr, y_scr, st_scr,
                *, lc, n_sub, reverse):
    n = pl.program_id(1)

    @pl.when(n == 0)
    def _():
        st_scr[...] = s0_ref[...]

    steps = lc // n_sub
    rows = steps * SUBLANES
    ar = jnp.broadcast_to(a_ref[0:1, :], (SUBLANES, N_STATE))
    ai = jnp.broadcast_to(a_ref[1:2, :], (SUBLANES, N_STATE))

    def project_in(i):
        u = u_ref[i * steps:(i + 1) * steps].reshape(rows, W_GRP).astype(BF16)
        bu_scr[i * rows:(i + 1) * rows, :] = jnp.dot(u, bm_ref[...], preferred_element_type=F32)

    def recurrence(i, carry):
        sr, si = carry
        ts = range(steps - 1, -1, -1) if reverse else range(steps)
        for t in ts:
            r = slice(i * rows + t * SUBLANES, i * rows + (t + 1) * SUBLANES)
            nr = ar * sr - ai * si + bu_scr[r, 0:N_STATE]
            ni = ar * si + ai * sr + bu_scr[r, N_STATE:2 * N_STATE]
            bu_scr[r, 0:N_STATE] = nr
            bu_scr[r, N_STATE:2 * N_STATE] = ni
            sr, si = nr, ni
        return sr, si

    def project_out(i):
        s = bu_scr[i * rows:(i + 1) * rows, :].astype(BF16)
        y = jnp.dot(s, cm_ref[...], preferred_element_type=F32)
        for half in range(W_GRP // LANES):
            y_scr[half, i * rows:(i + 1) * rows, :] = y[:, half * LANES:(half + 1) * LANES]

    order = list(range(n_sub - 1, -1, -1) if reverse else range(n_sub))
    carry = (st_scr[:, 0:N_STATE], st_scr[:, N_STATE:2 * N_STATE])
    project_in(order[0])
    for j, i in enumerate(order):
        if j + 1 < n_sub:
            project_in(order[j + 1])
        carry = recurrence(i, carry)
        project_out(i)
    st_scr[:, 0:N_STATE] = carry[0]
    st_scr[:, N_STATE:2 * N_STATE] = carry[1]
    for b in range(SUBLANES):
        for half in range(W_GRP // LANES):
            c0 = b * W_GRP + half * LANES
            y_ref[:, c0:c0 + LANES] = y_scr[half, pl.ds(b, lc, stride=SUBLANES), :]

    @pl.when(n == pl.num_programs(1) - 1)
    def _():
        fin_ref[...] = st_scr[...]


def _ssm(u_t, bmat, cmat, a, s0, *, batch, seq, lc, reverse):
    nch = seq // lc
    chunk = (lambda n: nch - 1 - n) if reverse else (lambda n: n)
    const = lambda b, n: (0, 0)
    return pl.pallas_call(
        functools.partial(_ssm_kernel, lc=lc, n_sub=4, reverse=reverse),
        grid=(batch // SUBLANES, nch),
        in_specs=[
            pl.BlockSpec((lc, SUBLANES, W_GRP), lambda b, n: (chunk(n), b, 0)),
            pl.BlockSpec((W_GRP, 2 * N_STATE), const),
            pl.BlockSpec((2 * N_STATE, W_GRP), const),
            pl.BlockSpec((2, N_STATE), const),
            pl.BlockSpec((SUBLANES, 2 * N_STATE), lambda b, n: (b, 0)),
        ],
        out_specs=[
            pl.BlockSpec((lc, SUBLANES * W_GRP), lambda b, n: (chunk(n), b)),
            pl.BlockSpec((SUBLANES, 2 * N_STATE), lambda b, n: (b, 0)),
        ],
        out_shape=[
            jax.ShapeDtypeStruct((seq, batch * W_GRP), F32),
            jax.ShapeDtypeStruct((batch, 2 * N_STATE), F32),
        ],
        scratch_shapes=[
            pltpu.VMEM((lc * SUBLANES, 2 * N_STATE), F32),
            pltpu.VMEM((W_GRP // LANES, lc * SUBLANES, LANES), F32),
            pltpu.VMEM((SUBLANES, 2 * N_STATE), F32),
        ],
        compiler_params=_cparams(("parallel", "arbitrary")),
        name="ssm_scan",
    )(u_t, bmat, cmat, a, s0)


def _dft_kernel(t1c_ref, t1s_ref, t2c_ref, t2s_ref, pc_ref, ps_ref, o_ref, lc_scr, ls_scr, acc_scr,
                *, tm, scale):
    kb = pl.program_id(2)
    t2c = t2c_ref[...]
    t2s = t2s_ref[...]
    for a in range(tm // DFT_RADIX):
        c1 = t1c_ref[a:a + 1, :]
        s1 = t1s_ref[a:a + 1, :]
        rows = slice(a * DFT_RADIX, (a + 1) * DFT_RADIX)
        lc_scr[rows, :] = (c1 * t2c - s1 * t2s).astype(BF16)
        ls_scr[rows, :] = (s1 * t2c + c1 * t2s).astype(BF16)
    part = (jnp.dot(lc_scr[...], pc_ref[...], preferred_element_type=F32)
            - jnp.dot(ls_scr[...], ps_ref[...], preferred_element_type=F32))

    @pl.when(kb == 0)
    def _():
        acc_scr[...] = part

    @pl.when(kb > 0)
    def _():
        acc_scr[...] += part

    @pl.when(kb == pl.num_programs(2) - 1)
    def _():
        o_ref[...] = (acc_scr[...] * scale).astype(BF16)


def _dft(tabs, pc, ps, *, seq, tm, tk, tn):
    nc = pc.shape[1]
    t1c, t1s, t2c, t2s = tabs
    ra = tm // DFT_RADIX
    scale = 1.0 / math.sqrt(seq * W_GRP)
    return pl.pallas_call(
        functools.partial(_dft_kernel, tm=tm, scale=scale),
        grid=(seq // tm, nc // tn, seq // tk),
        in_specs=[
            pl.BlockSpec((ra, tk), lambda i, j, k: (i, k)),
            pl.BlockSpec((ra, tk), lambda i, j, k: (i, k)),
            pl.BlockSpec((DFT_RADIX, tk), lambda i, j, k: (0, k)),
            pl.BlockSpec((DFT_RADIX, tk), lambda i, j, k: (0, k)),
            pl.BlockSpec((tk, tn), lambda i, j, k: (k, j)),
            pl.BlockSpec((tk, tn), lambda i, j, k: (k, j)),
        ],
        out_specs=pl.BlockSpec((tm, tn), lambda i, j, k: (i, j)),
        out_shape=jax.ShapeDtypeStruct((seq, nc), BF16),
        scratch_shapes=[pltpu.VMEM((tm, tk), BF16), pltpu.VMEM((tm, tk), BF16), pltpu.VMEM((tm, tn), F32)],
        compiler_params=_cparams(("parallel", "parallel", "arbitrary")),
        name="pos_dft",
    )(t1c, t1s, t2c, t2s, pc, ps)


def _mix_out_kernel(h_ref, mod_ref, yf_ref, yb_ref, u_ref, att_ref, f_ref, d_ref, wglu_ref, bglu_ref,
                    wf_ref, bf_ref, wo_ref, o_ref, m_scr):
    y = d_ref[...] * u_ref[...] + yf_ref[...] + yb_ref[...]
    y = y * (0.5 * (1.0 + jnp.tanh(math.sqrt(2.0 / math.pi) * (y + 0.044715 * (y * y * y)))))
    z = jnp.dot(y.astype(BF16), wglu_ref[...], preferred_element_type=F32) + bglu_ref[...]
    m_scr[:, 0:256] = (y * jax.nn.sigmoid(z)).astype(BF16)
    m_scr[:, 256:768] = att_ref[...]
    m_scr[:, 768:1024] = (jnp.dot(f_ref[...], wf_ref[...], preferred_element_type=F32) + bf_ref[...]).astype(BF16)
    mixed = jnp.dot(m_scr[...], wo_ref[...], preferred_element_type=F32)
    o_ref[...] = h_ref[...] + mod_ref[0, 5:6, :] * mixed


def _mix_out(h, mod, y_fwd, y_bwd, u_t, att, f, d, wglu, bglu, wf, bfn, wo, *, batch, seq, tm, cond_base,
             per_batch_cond):
    t = batch * seq
    tiles = seq // tm
    cond = lambda i: (cond_base + (i // tiles if per_batch_cond else 0), 0, 0)
    tcol = lambda i: (i % tiles, i // tiles)
    const = lambda i: (0, 0)
    return pl.pallas_call(
        _mix_out_kernel,
        grid=(t // tm,),
        in_specs=[
            pl.BlockSpec((tm, D_MODEL), lambda i: (i, 0)),
            pl.BlockSpec((1, N_MOD, D_MODEL), cond),
            pl.BlockSpec((tm, W_GRP), tcol),
            pl.BlockSpec((tm, W_GRP), tcol),
            pl.BlockSpec((tm, W_GRP), tcol),
            pl.BlockSpec((tm, 512), lambda i: (i, 0)),
            pl.BlockSpec((tm, W_GRP), tcol),
            pl.BlockSpec((1, W_GRP), const),
            pl.BlockSpec((W_GRP, W_GRP), const),
            pl.BlockSpec((1, W_GRP), const),
            pl.BlockSpec((W_GRP, W_GRP), const),
            pl.BlockSpec((1, W_GRP), const),
            pl.BlockSpec((D_MODEL, D_MODEL), const),
        ],
        out_specs=pl.BlockSpec((tm, D_MODEL), lambda i: (i, 0)),
        out_shape=jax.ShapeDtypeStruct((t, D_MODEL), F32),
        scratch_shapes=[pltpu.VMEM((tm, D_MODEL), BF16)],
        compiler_params=_cparams(("parallel",)),
        name="mix_out",
    )(h, mod, y_fwd, y_bwd, u_t, att, f, d, wglu, bglu, wf, bfn, wo)


def _rope_tables(seq):
    rows = seq // GRID_W
    row_id = jnp.repeat(jnp.arange(rows), GRID_W).astype(F32)
    col_id = jnp.tile(jnp.arange(GRID_W), rows).astype(F32)
    n_freq = HEAD_DIM // 4
    inv = ROPE_BASE ** (-jnp.arange(n_freq, dtype=F32) / n_freq)
    ang = jnp.concatenate([row_id[:, None] * inv, col_id[:, None] * inv], axis=-1)
    cos, sin = jnp.cos(ang), jnp.sin(ang)
    cos_full = jnp.tile(jnp.concatenate([cos, cos], axis=-1), (1, N_Q_HEADS))
    sin_signed = jnp.tile(jnp.concatenate([-sin, sin], axis=-1), (1, N_Q_HEADS))
    return cos_full, sin_signed


def _angle_table(mult, n, period):
    m = (mult[:, None] * n[None, :]) % period
    th = m.astype(F32) * (2.0 * math.pi / period)
    return jnp.cos(th), jnp.sin(th)


def _dft_tables(seq):
    n = jnp.arange(seq, dtype=jnp.int32)
    t1c, t1s = _angle_table(DFT_RADIX * jnp.arange(seq // DFT_RADIX, dtype=jnp.int32), n, seq)
    t2c, t2s = _angle_table(jnp.arange(DFT_RADIX, dtype=jnp.int32), n, seq)
    return t1c, t1s, t2c, t2s


def _ssm_params(lam_re, lam_im, b_re, b_im, c_re, c_im, log_dt):
    dt = jnp.exp(log_dt)[..., None]
    mag = jnp.exp(lam_re * dt)
    ar = mag * jnp.cos(lam_im * dt)
    ai = mag * jnp.sin(lam_im * dt)
    den = lam_re * lam_re + lam_im * lam_im
    qr = ((ar - 1.0) * lam_re + ai * lam_im) / den
    qi = (ai * lam_re - (ar - 1.0) * lam_im) / den
    bb_re = qr[..., None] * b_re - qi[..., None] * b_im
    bb_im = qr[..., None] * b_im + qi[..., None] * b_re
    eye = jnp.eye(N_SSM_GROUPS, dtype=F32)

    def block_diag(x):
        y = jnp.transpose(x, (0, 1, 3, 2))[:, :, :, None, :] * eye[None, :, None, :, None]
        return y.reshape(2, x.shape[1] * x.shape[3], x.shape[1] * x.shape[2])

    bmat = jnp.concatenate([block_diag(bb_re), block_diag(bb_im)], axis=-1).astype(BF16)
    cmat = jnp.concatenate([block_diag(c_re), block_diag(-c_im)], axis=1).astype(BF16)
    a2 = jnp.stack([ar.reshape(2, N_STATE), ai.reshape(2, N_STATE)], axis=1)
    return a2, bmat, cmat


def _permute_q_heads(w_in):
    cols = jnp.arange(P_IN, dtype=jnp.int32)
    for base in (256, 768):
        blk = jnp.concatenate([base + h * HEAD_DIM + jnp.arange(HEAD_DIM, dtype=jnp.int32) for h in Q_HEAD_ORDER])
        cols = cols.at[base:base + N_Q_HEADS * HEAD_DIM].set(blk)
    return jnp.take(w_in, cols, axis=1)


def _cache_layout(cache):
    b, _, past, nkv, dh = cache.shape
    keys = cache[:, 0].reshape(b, past, nkv * dh).astype(BF16)
    vt = jnp.transpose(cache[:, 1], (0, 2, 3, 1))
    ones = jnp.ones((b, nkv, VT_ROWS - dh, past), vt.dtype)
    vt = jnp.concatenate([vt, ones], axis=2).reshape(b, nkv * VT_ROWS, past).astype(BF16)
    return keys, vt


def kernel(x_prompt, x_sample, cache_swa_kv, cache_axial_kv, state_ssm, c, c_ctx, w_mod, b_mod, norm_ffn1, norm_mix, norm_ffn2, ffn1_w_gate, ffn1_w_up, ffn1_w_down, ffn2_w_gate, ffn2_w_up, ffn2_w_down, w_in, w_out, ssm_lambda_re, ssm_lambda_im, ssm_b_re, ssm_b_im, ssm_c_re, ssm_c_im, ssm_log_dt, ssm_d, ssm_w_glu, ssm_b_glu, swa_sink, ax_q_norm, ax_k_norm, fnet_w, fnet_b, final_norm):
    depth = w_mod.shape[0]
    cb, cl, _ = x_prompt.shape
    lb, ll, _ = x_sample.shape

    cond = jnp.zeros((16, D_MODEL), F32).at[0].set(c_ctx).at[1:1 + lb].set(c)
    mod_all = _modulation(cond, w_mod, b_mod.reshape(depth, 1, N_MOD * D_MODEL))
    mod_all = mod_all.reshape(depth, 16, N_MOD, D_MODEL)

    e_heads = jnp.kron(jnp.eye(256 // HEAD_DIM, dtype=F32), jnp.ones((HEAD_DIM, HEAD_DIM), F32)) / HEAD_DIM
    e_heads = e_heads.astype(BF16)
    kc = jnp.arange(W_GRP, dtype=jnp.int32)
    cc, sc = _angle_table(kc, kc, W_GRP)
    cs_chan = jnp.concatenate([cc, sc], axis=-1).astype(BF16)
    rope_tabs = _rope_tables(ll)
    dft_ctx = _dft_tables(cl)
    dft_lat = _dft_tables(ll)
    fg = final_norm.reshape(1, D_MODEL)

    h_ctx = x_prompt.reshape(cb * cl, D_MODEL)
    h_lat = x_sample.reshape(lb * ll, D_MODEL)
    tm_lat = 512
    swa_list, ax_list, ssm_list = [], [], []
    ffn1_w = tuple(w.astype(BF16) for w in (ffn1_w_gate, ffn1_w_up, ffn1_w_down))
    ffn2_w = tuple(w.astype(BF16) for w in (ffn2_w_gate, ffn2_w_up, ffn2_w_down))
    for l in range(depth):
        mod = mod_all[l]
        bf = lambda w: w[l].astype(BF16)
        row = lambda v: v[l].reshape(1, -1)
        ffn1 = (row(norm_ffn1), *ffn1_w, fg)
        ffn2 = (row(norm_ffn2), *ffn2_w, fg)
        w_in_l, w_out_l = _permute_q_heads(w_in[l]).astype(BF16), bf(w_out)
        qn = jnp.tile(ax_q_norm[l], N_Q_HEADS).reshape(1, 256)
        kn = jnp.tile(ax_k_norm[l], KV_W // HEAD_DIM).reshape(1, KV_W)
        a2, bmat, cmat = _ssm_params(ssm_lambda_re[l], ssm_lambda_im[l], ssm_b_re[l], ssm_b_im[l],
                                     ssm_c_re[l], ssm_c_im[l], ssm_log_dt[l])
        mix_out_w = (row(ssm_d), bf(ssm_w_glu), row(ssm_b_glu), bf(fnet_w), row(fnet_b), w_out_l)
        sink = swa_sink[l]
        last = l == depth - 1

        h_ctx = _ffn(h_ctx, mod, *ffn1, layer=l, mod_base=0, tm=512, tiles_per_cond=1 << 30, cond_base=0,
                     final_norm=False)
        u_t, qkv, pc, ps, kv_s, kv_a = _mix_in(h_ctx, mod, row(norm_mix), w_in_l, qn, kn, e_heads, cs_chan, None,
                                               batch=cb, seq=cl, tm=cl, cond_base=0, rope=False)
        u3 = u_t.reshape(cl, cb, W_GRP)
        zero_state = jnp.zeros((cb, 2 * N_STATE), F32)
        y_dirs, s_fins = zip(*[_ssm(u3, bmat[d], cmat[d], a2[d], zero_state, batch=cb, seq=cl, lc=128,
                                    reverse=bool(d)) for d in range(2)])
        att = _attn_ctx(sink, qkv, batch=cb, seq=cl)
        f = _dft(dft_ctx, pc, ps, seq=cl, tm=cl, tk=cl, tn=2048)
        h_ctx = _mix_out(h_ctx, mod, *y_dirs, u_t, att, f, *mix_out_w,
                         batch=cb, seq=cl, tm=cl, cond_base=0, per_batch_cond=False)
        h_ctx = _ffn(h_ctx, mod, *ffn2, layer=l, mod_base=6, tm=512, tiles_per_cond=1 << 30, cond_base=0,
                     final_norm=last)
        swa_list.append(kv_s.reshape(cb, 2, cl, KV_W // HEAD_DIM, HEAD_DIM))
        ax_list.append(kv_a.reshape(cb, 2, cl, KV_W // HEAD_DIM, HEAD_DIM))
        s_fin = jnp.stack(s_fins, axis=0).reshape(2, cb, 2, N_SSM_GROUPS, SSM_STATE)
        ssm_list.append(jnp.transpose(s_fin, (1, 0, 3, 4, 2)))

        tpc = ll // tm_lat
        h_lat = _ffn(h_lat, mod, *ffn1, layer=l, mod_base=0, tm=tm_lat, tiles_per_cond=tpc, cond_base=1,
                     final_norm=False)
        u_t, qkv, pc, ps, vt = _mix_in(h_lat, mod, row(norm_mix), w_in_l, qn, kn, e_heads, cs_chan, rope_tabs,
                                       batch=lb, seq=ll, tm=tm_lat, cond_base=1, rope=True)
        s0 = jnp.transpose(state_ssm[:, l], (1, 0, 4, 2, 3)).reshape(2, lb, 2 * N_STATE)
        u3 = u_t.reshape(ll, lb, W_GRP)
        y_dirs = [_ssm(u3, bmat[d], cmat[d], a2[d], s0[d], batch=lb, seq=ll, lc=128, reverse=bool(d))[0]
                  for d in range(2)]
        kc_s, vc_s = _cache_layout(cache_swa_kv[:, l])
        kc_a, vc_a = _cache_layout(cache_axial_kv[:, l])
        att = _attn_lat(sink, qkv, vt, kc_s, vc_s, kc_a, vc_a, batch=lb, seq=ll, tq=256)
        f = _dft(dft_lat, pc, ps, seq=ll, tm=512, tk=1024, tn=2048)
        h_lat = _mix_out(h_lat, mod, *y_dirs, u_t, att, f, *mix_out_w,
                         batch=lb, seq=ll, tm=tm_lat, cond_base=1, per_batch_cond=True)
        h_lat = _ffn(h_lat, mod, *ffn2, layer=l, mod_base=6, tm=tm_lat, tiles_per_cond=tpc, cond_base=1,
                     final_norm=last)

    y_prompt = h_ctx.reshape(cb, cl, D_MODEL)
    y_sample = h_lat.reshape(lb, ll, D_MODEL)
    return (y_prompt, y_sample, jnp.stack(swa_list, axis=1), jnp.stack(ax_list, axis=1),
            jnp.stack(ssm_list, axis=1))
```

```python
import functools
import math

import jax
import jax.numpy as jnp
from jax import lax
from jax.experimental import pallas as pl
from jax.experimental.pallas import tpu as pltpu

F32 = jnp.float32
BF16 = jnp.bfloat16

D_MODEL = 1024
D_FF = 2816
N_MOD = 9
EPS = 1e-6
HEAD_DIM = 64
HALF_HEAD = HEAD_DIM // 2
N_Q_HEADS = 4
Q_PER_KV = 2
W_GRP = 256
KV_W = 128
P_IN = 1536
N_SSM_GROUPS = 16
SSM_GROUP = 16
SSM_STATE = 64
N_STATE = N_SSM_GROUPS * SSM_STATE
WINDOW = 128
GRID_W = 64
ROPE_BASE = 10000.0
NEG_BIG = -1e30
LOG2E = math.log2(math.e)
Q_HEAD_ORDER = (0, 2, 1, 3)
N_KV_HEADS = N_Q_HEADS // Q_PER_KV
VT_ROWS = 80

FF_CHUNKS = (1024, 1024, 768)
SUBLANES = 8
LANES = 128
DFT_RADIX = 64
VMEM_LIMIT = 56 * 1024 * 1024


def _cparams(sem):
    return pltpu.CompilerParams(dimension_semantics=sem, vmem_limit_bytes=VMEM_LIMIT)


def _adaln(h, g, sc, sh):
    ms = jnp.mean(h * h, axis=-1, keepdims=True)
    return (h * lax.rsqrt(ms + EPS) * g) * (1.0 + sc) + sh


def _mod_kernel(c_ref, w_ref, b_ref, o_ref):
    c = c_ref[...]
    a = (c * jax.nn.sigmoid(c)).astype(BF16)
    o_ref[...] = jnp.dot(a, w_ref[...].astype(BF16), preferred_element_type=F32) + b_ref[...]


def _modulation(cond, w_mod, b_mod):
    depth = w_mod.shape[0]
    n = N_MOD * D_MODEL
    tn = 1024
    return pl.pallas_call(
        _mod_kernel,
        grid=(depth, n // tn),
        in_specs=[
            pl.BlockSpec((16, D_MODEL), lambda l, j: (0, 0)),
            pl.BlockSpec((None, D_MODEL, tn), lambda l, j: (l, 0, j)),
            pl.BlockSpec((None, 1, tn), lambda l, j: (l, 0, j)),
        ],
        out_specs=pl.BlockSpec((None, 16, tn), lambda l, j: (l, 0, j)),
        out_shape=jax.ShapeDtypeStruct((depth, 16, n), F32),
        compiler_params=_cparams(("parallel", "parallel")),
        name="modulation",
    )(cond, w_mod, b_mod)


def _ffn_kernel(h_ref, mod_ref, g_ref, wg_ref, wu_ref, wd_ref, fg_ref, o_ref, *, mod_base, final_norm):
    h = h_ref[...]
    x = _adaln(h, g_ref[...], mod_ref[0, mod_base + 1:mod_base + 2, :],
               mod_ref[0, mod_base:mod_base + 1, :]).astype(BF16)
    acc = None
    c0 = 0
    for width in FF_CHUNKS:
        gate = jnp.dot(x, wg_ref[:, c0:c0 + width], preferred_element_type=F32)
        up = jnp.dot(x, wu_ref[:, c0:c0 + width], preferred_element_type=F32)
        hg = 0.5 * gate
        a = (hg * (1.0 + jnp.tanh(hg)) * up).astype(BF16)
        part = jnp.dot(a, wd_ref[c0:c0 + width, :], preferred_element_type=F32)
        acc = part if acc is None else acc + part
        c0 += width
    hn = h + (0.5 * mod_ref[0, mod_base + 2:mod_base + 3, :]) * acc
    if final_norm:
        ms = jnp.mean(hn * hn, axis=-1, keepdims=True)
        hn = hn * lax.rsqrt(ms + EPS) * fg_ref[...]
    o_ref[...] = hn


def _ffn(h, mod, g, wg, wu, wd, fg, *, layer, mod_base, tm, tiles_per_cond, cond_base, final_norm):
    t = h.shape[0]
    cond = lambda i: (cond_base + i // tiles_per_cond, 0, 0)
    const = lambda i: (0, 0)
    of_layer = lambda i: (layer, 0, 0)
    resident = dict(pipeline_mode=pl.Buffered(1))
    kern = functools.partial(_ffn_kernel, mod_base=mod_base, final_norm=final_norm)
    return pl.pallas_call(
        kern,
        grid=(t // tm,),
        in_specs=[
            pl.BlockSpec((tm, D_MODEL), lambda i: (i, 0)),
            pl.BlockSpec((1, N_MOD, D_MODEL), cond),
            pl.BlockSpec((1, D_MODEL), const),
            pl.BlockSpec((None, D_MODEL, D_FF), of_layer, **resident),
            pl.BlockSpec((None, D_MODEL, D_FF), of_layer, **resident),
            pl.BlockSpec((None, D_FF, D_MODEL), of_layer, **resident),
            pl.BlockSpec((1, D_MODEL), const),
        ],
        out_specs=pl.BlockSpec((tm, D_MODEL), lambda i: (i, 0)),
        out_shape=jax.ShapeDtypeStruct((t, D_MODEL), F32),
        compiler_params=_cparams(("parallel",)),
        name="ffn",
    )(h, mod, g, wg, wu, wd, fg)


def _head_mean_sq(x, e):
    x2 = x * x
    hi = x2.astype(BF16)
    lo = (x2 - hi.astype(F32)).astype(BF16)
    return jnp.dot(hi, e, preferred_element_type=F32) + jnp.dot(lo, e, preferred_element_type=F32)


def _swap_halves(x):
    w = x.shape[-1]
    lane = lax.broadcasted_iota(jnp.int32, x.shape, 1)
    first = (lane & HALF_HEAD) == 0
    return jnp.where(first, pltpu.roll(x, w - HALF_HEAD, 1), pltpu.roll(x, HALF_HEAD, 1))


def _mix_in_kernel(*refs, rope):
    if rope:
        (h_ref, mod_ref, g_ref, w_ref, qn_ref, kn_ref, e_ref, cs_ref, cos_ref, sin_ref,
         u_ref, qkv_ref, pc_ref, ps_ref, vt_ref) = refs
    else:
        (h_ref, mod_ref, g_ref, w_ref, qn_ref, kn_ref, e_ref, cs_ref,
         u_ref, qkv_ref, pc_ref, ps_ref, vt_ref, kvs_ref, kva_ref) = refs
    x = _adaln(h_ref[...], g_ref[...], mod_ref[0, 4:5, :], mod_ref[0, 3:4, :]).astype(BF16)
    proj = jnp.dot(x, w_ref[...], preferred_element_type=F32)
    u_ssm = proj[:, 0:256]
    q_s = proj[:, 256:512]
    k_s = proj[:, 512:640]
    v_s = proj[:, 640:768]
    q_a = proj[:, 768:1024]
    k_a = proj[:, 1024:1152]
    v_a = proj[:, 1152:1280]
    u_f = proj[:, 1280:1536]

    e = e_ref[...]
    q_a = q_a * lax.rsqrt(_head_mean_sq(q_a, e) + EPS) * qn_ref[...]
    k_a = k_a * lax.rsqrt(_head_mean_sq(k_a, e[0:KV_W, 0:KV_W]) + EPS) * kn_ref[...]

    u_ref[...] = u_ssm
    p = jnp.dot(u_f.astype(BF16), cs_ref[...], preferred_element_type=F32)
    pc_ref[...] = p[:, 0:256].astype(BF16)
    ps_ref[...] = p[:, 256:512].astype(BF16)

    scale = LOG2E * HEAD_DIM ** -0.5
    ones = jnp.ones((VT_ROWS - HEAD_DIM, v_s.shape[0]), BF16)
    for grp, v in enumerate((v_s, v_a)):
        vt = jnp.transpose(v).astype(BF16)
        for kv in range(N_KV_HEADS):
            r0 = (grp * N_KV_HEADS + kv) * VT_ROWS
            vt_ref[r0:r0 + HEAD_DIM, :] = vt[kv * HEAD_DIM:(kv + 1) * HEAD_DIM, :]
            vt_ref[r0 + HEAD_DIM:r0 + VT_ROWS, :] = ones
    if rope:
        cos = cos_ref[...]
        sin = sin_ref[...]
        rot = lambda t, w: t * cos[:, 0:w] + _swap_halves(t) * sin[:, 0:w]
        qkv_ref[:, 0:256] = (q_s * scale).astype(BF16)
        qkv_ref[:, 256:512] = (rot(q_s, 256) * scale).astype(BF16)
        qkv_ref[:, 512:768] = (q_a * scale).astype(BF16)
        qkv_ref[:, 768:1024] = (rot(q_a, 256) * scale).astype(BF16)
        qkv_ref[:, 1024:1152] = rot(k_s, KV_W).astype(BF16)
        qkv_ref[:, 1152:1280] = rot(k_a, KV_W).astype(BF16)
    else:
        qkv_ref[:, 0:256] = (q_s * scale).astype(BF16)
        qkv_ref[:, 256:512] = (q_a * scale).astype(BF16)
        qkv_ref[:, 512:640] = k_s.astype(BF16)
        qkv_ref[:, 640:768] = k_a.astype(BF16)
        kvs_ref[0] = k_s
        kvs_ref[1] = v_s
        kva_ref[0] = k_a
        kva_ref[1] = v_a


def _mix_in(h, mod, g, w_in, qn, kn, e, cs, rope_tabs, *, batch, seq, tm, cond_base, rope):
    t = batch * seq
    tiles = seq // tm
    nc = batch * W_GRP
    cond = lambda i: (cond_base + (i // tiles if rope else 0), 0, 0)
    tcol = lambda i: (i % tiles, i // tiles)
    const = lambda i: (0, 0)
    in_specs = [
        pl.BlockSpec((tm, D_MODEL), lambda i: (i, 0)),
        pl.BlockSpec((1, N_MOD, D_MODEL), cond),
        pl.BlockSpec((1, D_MODEL), const),
        pl.BlockSpec((D_MODEL, P_IN), const),
        pl.BlockSpec((1, 256), const),
        pl.BlockSpec((1, KV_W), const),
        pl.BlockSpec((256, 256), const),
        pl.BlockSpec((256, 512), const),
    ]
    args = [h, mod, g, w_in, qn, kn, e, cs]
    qkv_w = 1280 if rope else 768
    out_specs = [
        pl.BlockSpec((tm, W_GRP), tcol),
        pl.BlockSpec((tm, qkv_w), lambda i: (i, 0)),
        pl.BlockSpec((tm, W_GRP), tcol),
        pl.BlockSpec((tm, W_GRP), tcol),
        pl.BlockSpec((None, 2 * N_KV_HEADS * VT_ROWS, tm), lambda i: (i // tiles, 0, i % tiles)),
    ]
    out_shape = [
        jax.ShapeDtypeStruct((seq, nc), F32),
        jax.ShapeDtypeStruct((t, qkv_w), BF16),
        jax.ShapeDtypeStruct((seq, nc), BF16),
        jax.ShapeDtypeStruct((seq, nc), BF16),
        jax.ShapeDtypeStruct((batch, 2 * N_KV_HEADS * VT_ROWS, seq), BF16),
    ]
    if rope:
        in_specs += [pl.BlockSpec((tm, 256), lambda i: (i % tiles, 0))] * 2
        args += list(rope_tabs)
    else:
        assert tm == seq
        out_specs += [pl.BlockSpec((None, 2, seq, KV_W), lambda i: (i, 0, 0, 0))] * 2
        out_shape += [jax.ShapeDtypeStruct((batch, 2, seq, KV_W), F32)] * 2
    return pl.pallas_call(
        functools.partial(_mix_in_kernel, rope=rope),
        grid=(t // tm,),
        in_specs=in_specs,
        out_specs=out_specs,
        out_shape=out_shape,
        compiler_params=_cparams(("parallel",)),
        name="mix_in",
    )(*args)


def _qk(a, b):
    return lax.dot_general(a, b, (((1,), (1,)), ((), ())), preferred_element_type=F32)


def _attn_ctx_kernel(sink_ref, qkv_ref, vt_ref, o_ref, *, seq):
    lane = lax.broadcasted_iota(jnp.int32, (Q_PER_KV * seq, KV_W), 1)
    for grp in range(2):
        q = jnp.concatenate([qkv_ref[:, grp * 256:grp * 256 + KV_W],
                             qkv_ref[:, grp * 256 + KV_W:(grp + 1) * 256]], axis=0)
        k = qkv_ref[:, 512 + grp * KV_W:512 + (grp + 1) * KV_W]
        outs = []
        for kv in range(N_KV_HEADS):
            keep = (lane < HEAD_DIM) if kv == 0 else (lane >= HEAD_DIM)
            s = _qk(k, jnp.where(keep, q, 0))
            m = jnp.max(s, axis=0, keepdims=True)
            if grp == 0:
                sink = jnp.concatenate(
                    [jnp.full((1, seq), sink_ref[Q_PER_KV * kv + j] * LOG2E, F32) for j in range(Q_PER_KV)], axis=1)
                m = jnp.maximum(m, sink)
            p = jnp.exp2((s - m).astype(BF16))
            r0 = (grp * N_KV_HEADS + kv) * VT_ROWS
            ot = jnp.dot(vt_ref[r0:r0 + VT_ROWS, :], p, preferred_element_type=F32)
            den = ot[HEAD_DIM:HEAD_DIM + 1, :]
            if grp == 0:
                den = den + jnp.exp2(sink - m)
            o = ot[0:HEAD_DIM, :] / den
            outs += [o[:, j * seq:(j + 1) * seq] for j in range(Q_PER_KV)]
        o_ref[:, grp * 256:(grp + 1) * 256] = jnp.transpose(jnp.concatenate(outs, axis=0)).astype(BF16)


def _attn_ctx(sink, qkv, vt, *, batch, seq):
    return pl.pallas_call(
        functools.partial(_attn_ctx_kernel, seq=seq),
        grid=(batch,),
        in_specs=[
            pl.BlockSpec(memory_space=pltpu.SMEM),
            pl.BlockSpec((seq, 768), lambda b: (b, 0)),
            pl.BlockSpec((None, 2 * N_KV_HEADS * VT_ROWS, seq), lambda b: (b, 0, 0)),
        ],
        out_specs=pl.BlockSpec((seq, 512), lambda b: (b, 0)),
        out_shape=jax.ShapeDtypeStruct((batch * seq, 512), BF16),
        compiler_params=_cparams(("parallel",)),
        name="attn_ctx",
    )(sink, qkv, vt)


def _attn_lat_kernel(sink_ref, zero_ref, q_ref, k_ref, vt_ref, kcs_ref, vcs_ref, kca_ref, vca_ref, o_ref,
                     qm_scr, m_scr, acc_scr, s0_scr, s1_scr, bm_scr, *, tq, seq, kc):
    i = pl.program_id(1)
    q0 = i * tq
    win = tq + 2 * WINDOW
    past = kcs_ref.shape[0]
    assert win <= kc and past <= kc
    ws = pl.multiple_of(jnp.clip(q0 - WINDOW, 0, seq - win), WINDOW)
    lane = lax.broadcasted_iota(jnp.int32, (Q_PER_KV * tq, KV_W), 1)

    def banded(s):
        kpos = ws + lax.broadcasted_iota(jnp.int32, s.shape, 0)
        qpos = q0 + (lax.broadcasted_iota(jnp.int32, s.shape, 1) & (tq - 1))
        return jnp.where(jnp.abs(qpos - kpos) <= WINDOW, s, NEG_BIG)

    def sink_row(kv):
        return jnp.concatenate(
            [jnp.full((1, tq), sink_ref[Q_PER_KV * kv + j] * LOG2E, F32) for j in range(Q_PER_KV)], axis=1)

    def vt_rows(grp, kv):
        r0 = (grp * N_KV_HEADS + kv) * VT_ROWS
        return slice(r0, r0 + VT_ROWS)

    acc_scr[...] = jnp.zeros(acc_scr.shape, F32)
    for grp in range(2):
        base = grp * 512
        q_plain = jnp.concatenate([q_ref[:, base:base + KV_W], q_ref[:, base + KV_W:base + 2 * KV_W]], axis=0)
        q_rope = jnp.concatenate([q_ref[:, base + 256:base + 256 + KV_W],
                                  q_ref[:, base + 256 + KV_W:base + 256 + 2 * KV_W]], axis=0)
        for kv in range(N_KV_HEADS):
            keep = (lane < HEAD_DIM) if kv == 0 else (lane >= HEAD_DIM)
            qm_scr[grp, 0, kv] = jnp.where(keep, q_plain, 0)
            qm_scr[grp, 1, kv] = jnp.where(keep, q_rope, 0)
            m_scr[grp, kv] = sink_row(kv) if grp == 0 else jnp.full((1, Q_PER_KV * tq), NEG_BIG, F32)

    jobs = [
        (0, past, lambda kv: _qk(kcs_ref[...], qm_scr[0, 0, kv]),
         lambda kv: vcs_ref[kv * VT_ROWS:(kv + 1) * VT_ROWS, :]),
        (0, win, lambda kv: banded(_qk(k_ref[pl.ds(ws, win), 0:KV_W], qm_scr[0, 1, kv])),
         lambda kv: vt_ref[vt_rows(0, kv), pl.ds(ws, win)]),
        (1, past, lambda kv: _qk(kca_ref[...], qm_scr[1, 0, kv]),
         lambda kv: vca_ref[kv * VT_ROWS:(kv + 1) * VT_ROWS, :]),
    ]
    for c in range(seq // kc):
        jobs.append((1, kc, lambda kv, c=c: _qk(k_ref[c * kc:(c + 1) * kc, KV_W:2 * KV_W], qm_scr[1, 1, kv]),
                     lambda kv, c=c: vt_ref[vt_rows(1, kv), c * kc:(c + 1) * kc]))

    s_slots = (s0_scr, s1_scr)
    row0 = pl.multiple_of(zero_ref[0], SUBLANES)

    def scores_to(slot, job):
        _, n, scores, _ = job
        for kv in range(N_KV_HEADS):
            s = scores(kv)
            s_slots[slot][kv, 0:n, :] = s
            bm_scr[slot, kv] = jnp.max(s, axis=0, keepdims=True)

    def update_from(slot, job):
        grp, n, _, values = job
        for kv in range(N_KV_HEADS):
            m_old = m_scr[grp, kv]
            m_new = jnp.maximum(m_old, bm_scr[slot, kv])
            p = jnp.exp2((s_slots[slot][kv, pl.ds(row0, n), :] - m_new).astype(BF16))
            acc_scr[grp, kv] = (jnp.exp2(m_old - m_new) * acc_scr[grp, kv]
                                + jnp.dot(values(kv), p, preferred_element_type=F32))
            m_scr[grp, kv] = m_new

    def finalize(grp):
        outs = []
        for kv in range(N_KV_HEADS):
            acc = acc_scr[grp, kv]
            den = acc[HEAD_DIM:HEAD_DIM + 1, :]
            if grp == 0:
                den = den + jnp.exp2(sink_row(kv) - m_scr[grp, kv])
            o = acc[0:HEAD_DIM, :] / den
            outs += [o[:, j * tq:(j + 1) * tq] for j in range(Q_PER_KV)]
        o_t = jnp.concatenate(outs, axis=0)
        o_ref[:, grp * 256:(grp + 1) * 256] = jnp.transpose(o_t).astype(BF16)

    scores_to(0, jobs[0])
    for j, job in enumerate(jobs):
        if j + 1 < len(jobs):
            scores_to((j + 1) % 2, jobs[j + 1])
        update_from(j % 2, job)
        if j + 1 == len(jobs) or jobs[j + 1][0] != job[0]:
            finalize(job[0])


def _attn_lat(sink, qkv, vt, kc_s, vc_s, kc_a, vc_a, *, batch, seq, tq, kc=512):
    nq = seq // tq
    past = kc_s.shape[1]
    per_b3 = lambda b, i: (b, 0, 0)
    return pl.pallas_call(
        functools.partial(_attn_lat_kernel, tq=tq, seq=seq, kc=kc),
        grid=(batch, nq),
        in_specs=[
            pl.BlockSpec(memory_space=pltpu.SMEM),
            pl.BlockSpec(memory_space=pltpu.SMEM),
            pl.BlockSpec((tq, 1024), lambda b, i: (b * nq + i, 0)),
            pl.BlockSpec((seq, 256), lambda b, i: (b, 4)),
            pl.BlockSpec((None, 2 * N_KV_HEADS * VT_ROWS, seq), per_b3),
            pl.BlockSpec((None, past, KV_W), per_b3),
            pl.BlockSpec((None, N_KV_HEADS * VT_ROWS, past), per_b3),
            pl.BlockSpec((None, past, KV_W), per_b3),
            pl.BlockSpec((None, N_KV_HEADS * VT_ROWS, past), per_b3),
        ],
        out_specs=pl.BlockSpec((tq, 512), lambda b, i: (b * nq + i, 0)),
        out_shape=jax.ShapeDtypeStruct((batch * seq, 512), BF16),
        scratch_shapes=[
            pltpu.VMEM((2, 2, N_KV_HEADS, Q_PER_KV * tq, KV_W), BF16),
            pltpu.VMEM((2, N_KV_HEADS, 1, Q_PER_KV * tq), F32),
            pltpu.VMEM((2, N_KV_HEADS, VT_ROWS, Q_PER_KV * tq), F32),
            pltpu.VMEM((N_KV_HEADS, kc, Q_PER_KV * tq), F32),
            pltpu.VMEM((N_KV_HEADS, kc, Q_PER_KV * tq), F32),
            pltpu.VMEM((2, N_KV_HEADS, 1, Q_PER_KV * tq), F32),
        ],
        compiler_params=_cparams(("parallel", "parallel")),
        name="attn_lat",
    )(sink, jnp.zeros((1,), jnp.int32), qkv, qkv, vt, kc_s, vc_s, kc_a, vc_a)


def _ssm_kernel(u_ref, bm_ref, cm_ref, a_ref, s0_ref, y_ref, fin_ref, bu_scr, y_scr, st_scr,
                *, lc, n_sub, reverse):
    n = pl.program_id(1)

    @pl.when(n == 0)
    def _():
        st_scr[...] = s0_ref[...]

    steps = lc // n_sub
    rows = steps * SUBLANES
    ar = jnp.broadcast_to(a_ref[0:1, :], (SUBLANES, N_STATE))
    ai = jnp.broadcast_to(a_ref[1:2, :], (SUBLANES, N_STATE))

    def project_in(i):
        u = u_ref[i * steps:(i + 1) * steps].reshape(rows, W_GRP).astype(BF16)
        bu_scr[i * rows:(i + 1) * rows, :] = jnp.dot(u, bm_ref[...], preferred_element_type=F32)

    def recurrence(i, carry):
        sr, si = carry
        ts = range(steps - 1, -1, -1) if reverse else range(steps)
        for t in ts:
            r = slice(i * rows + t * SUBLANES, i * rows + (t + 1) * SUBLANES)
            nr = ar * sr - ai * si + bu_scr[r, 0:N_STATE]
            ni = ar * si + ai * sr + bu_scr[r, N_STATE:2 * N_STATE]
            bu_scr[r, 0:N_STATE] = nr
            bu_scr[r, N_STATE:2 * N_STATE] = ni
            sr, si = nr, ni
        return sr, si

    def project_out(i):
        s = bu_scr[i * rows:(i + 1) * rows, :].astype(BF16)
        y = jnp.dot(s, cm_ref[...], preferred_element_type=F32)
        for half in range(W_GRP // LANES):
            y_scr[half, i * rows:(i + 1) * rows, :] = y[:, half * LANES:(half + 1) * LANES]

    order = list(range(n_sub - 1, -1, -1) if reverse else range(n_sub))
    carry = (st_scr[:, 0:N_STATE], st_scr[:, N_STATE:2 * N_STATE])
    project_in(order[0])
    for j, i in enumerate(order):
        if j + 1 < n_sub:
            project_in(order[j + 1])
        carry = recurrence(i, carry)
        project_out(i)
    st_scr[:, 0:N_STATE] = carry[0]
    st_scr[:, N_STATE:2 * N_STATE] = carry[1]
    for b in range(SUBLANES):
        for half in range(W_GRP // LANES):
            c0 = b * W_GRP + half * LANES
            y_ref[:, c0:c0 + LANES] = y_scr[half, pl.ds(b, lc, stride=SUBLANES), :]

    @pl.when(n == pl.num_programs(1) - 1)
    def _():
        fin_ref[...] = st_scr[...]


def _ssm(u_t, bmat, cmat, a, s0, *, batch, seq, lc, reverse):
    nch = seq // lc
    chunk = (lambda n: nch - 1 - n) if reverse else (lambda n: n)
    const = lambda b, n: (0, 0)
    return pl.pallas_call(
        functools.partial(_ssm_kernel, lc=lc, n_sub=4, reverse=reverse),
        grid=(batch // SUBLANES, nch),
        in_specs=[
            pl.BlockSpec((lc, SUBLANES, W_GRP), lambda b, n: (chunk(n), b, 0)),
            pl.BlockSpec((W_GRP, 2 * N_STATE), const),
            pl.BlockSpec((2 * N_STATE, W_GRP), const),
            pl.BlockSpec((2, N_STATE), const),
            pl.BlockSpec((SUBLANES, 2 * N_STATE), lambda b, n: (b, 0)),
        ],
        out_specs=[
            pl.BlockSpec((lc, SUBLANES * W_GRP), lambda b, n: (chunk(n), b)),
            pl.BlockSpec((SUBLANES, 2 * N_STATE), lambda b, n: (b, 0)),
        ],
        out_shape=[
            jax.ShapeDtypeStruct((seq, batch * W_GRP), F32),
            jax.ShapeDtypeStruct((batch, 2 * N_STATE), F32),
        ],
        scratch_shapes=[
            pltpu.VMEM((lc * SUBLANES, 2 * N_STATE), F32),
            pltpu.VMEM((W_GRP // LANES, lc * SUBLANES, LANES), F32),
            pltpu.VMEM((SUBLANES, 2 * N_STATE), F32),
        ],
        compiler_params=_cparams(("parallel", "arbitrary")),
        name="ssm_scan",
    )(u_t, bmat, cmat, a, s0)


def _dft_kernel(t1c_ref, t1s_ref, t2c_ref, t2s_ref, pc_ref, ps_ref, o_ref, *, tm, tk, scale):
    acc = None
    for kb in range(pc_ref.shape[0] // tk):
        ks = slice(kb * tk, (kb + 1) * tk)
        t2c = t2c_ref[:, ks]
        t2s = t2s_ref[:, ks]
        cos_rows, sin_rows = [], []
        for a in range(tm // DFT_RADIX):
            c1 = t1c_ref[a:a + 1, ks]
            s1 = t1s_ref[a:a + 1, ks]
            cos_rows.append((c1 * t2c - s1 * t2s).astype(BF16))
            sin_rows.append((s1 * t2c + c1 * t2s).astype(BF16))
        part = (jnp.dot(jnp.concatenate(cos_rows, axis=0), pc_ref[ks, :], preferred_element_type=F32)
                - jnp.dot(jnp.concatenate(sin_rows, axis=0), ps_ref[ks, :], preferred_element_type=F32))
        acc = part if acc is None else acc + part
    o_ref[...] = (acc * scale).astype(BF16)


def _dft(tabs, pc, ps, *, seq, tm, tk, tn):
    nc = pc.shape[1]
    t1c, t1s, t2c, t2s = tabs
    ra = tm // DFT_RADIX
    scale = 1.0 / math.sqrt(seq * W_GRP)
    resident = dict(pipeline_mode=pl.Buffered(1))
    return pl.pallas_call(
        functools.partial(_dft_kernel, tm=tm, tk=tk, scale=scale),
        grid=(nc // tn, seq // tm),
        in_specs=[
            pl.BlockSpec((ra, seq), lambda j, i: (i, 0)),
            pl.BlockSpec((ra, seq), lambda j, i: (i, 0)),
            pl.BlockSpec((DFT_RADIX, seq), lambda j, i: (0, 0), **resident),
            pl.BlockSpec((DFT_RADIX, seq), lambda j, i: (0, 0), **resident),
            pl.BlockSpec((seq, tn), lambda j, i: (0, j), **resident),
            pl.BlockSpec((seq, tn), lambda j, i: (0, j), **resident),
        ],
        out_specs=pl.BlockSpec((tm, tn), lambda j, i: (i, j)),
        out_shape=jax.ShapeDtypeStruct((seq, nc), BF16),
        compiler_params=_cparams(("parallel", "parallel")),
        name="pos_dft",
    )(t1c, t1s, t2c, t2s, pc, ps)


def _mix_out_kernel(h_ref, mod_ref, yf_ref, yb_ref, u_ref, att_ref, f_ref, d_ref, wglu_ref, bglu_ref,
                    wf_ref, bf_ref, wo_ref, o_ref, m_scr):
    y = d_ref[...] * u_ref[...] + yf_ref[...] + yb_ref[...]
    y = y * (0.5 * (1.0 + jnp.tanh(math.sqrt(2.0 / math.pi) * (y + 0.044715 * (y * y * y)))))
    z = jnp.dot(y.astype(BF16), wglu_ref[...], preferred_element_type=F32) + bglu_ref[...]
    m_scr[:, 0:256] = (y * jax.nn.sigmoid(z)).astype(BF16)
    m_scr[:, 256:768] = att_ref[...]
    m_scr[:, 768:1024] = (jnp.dot(f_ref[...], wf_ref[...], preferred_element_type=F32) + bf_ref[...]).astype(BF16)
    mixed = jnp.dot(m_scr[...], wo_ref[...], preferred_element_type=F32)
    o_ref[...] = h_ref[...] + mod_ref[0, 5:6, :] * mixed


def _mix_out(h, mod, y_fwd, y_bwd, u_t, att, f, d, wglu, bglu, wf, bfn, wo, *, batch, seq, tm, cond_base,
             per_batch_cond):
    t = batch * seq
    tiles = seq // tm
    cond = lambda i: (cond_base + (i // tiles if per_batch_cond else 0), 0, 0)
    tcol = lambda i: (i % tiles, i // tiles)
    const = lambda i: (0, 0)
    return pl.pallas_call(
        _mix_out_kernel,
        grid=(t // tm,),
        in_specs=[
            pl.BlockSpec((tm, D_MODEL), lambda i: (i, 0)),
            pl.BlockSpec((1, N_MOD, D_MODEL), cond),
            pl.BlockSpec((tm, W_GRP), tcol),
            pl.BlockSpec((tm, W_GRP), tcol),
            pl.BlockSpec((tm, W_GRP), tcol),
            pl.BlockSpec((tm, 512), lambda i: (i, 0)),
            pl.BlockSpec((tm, W_GRP), tcol),
            pl.BlockSpec((1, W_GRP), const),
            pl.BlockSpec((W_GRP, W_GRP), const),
            pl.BlockSpec((1, W_GRP), const),
            pl.BlockSpec((W_GRP, W_GRP), const),
            pl.BlockSpec((1, W_GRP), const),
            pl.BlockSpec((D_MODEL, D_MODEL), const),
        ],
        out_specs=pl.BlockSpec((tm, D_MODEL), lambda i: (i, 0)),
        out_shape=jax.ShapeDtypeStruct((t, D_MODEL), F32),
        scratch_shapes=[pltpu.VMEM((tm, D_MODEL), BF16)],
        compiler_params=_cparams(("parallel",)),
        name="mix_out",
    )(h, mod, y_fwd, y_bwd, u_t, att, f, d, wglu, bglu, wf, bfn, wo)


def _rope_tables(seq):
    rows = seq // GRID_W
    row_id = jnp.repeat(jnp.arange(rows), GRID_W).astype(F32)
    col_id = jnp.tile(jnp.arange(GRID_W), rows).astype(F32)
    n_freq = HEAD_DIM // 4
    inv = ROPE_BASE ** (-jnp.arange(n_freq, dtype=F32) / n_freq)
    ang = jnp.concatenate([row_id[:, None] * inv, col_id[:, None] * inv], axis=-1)
    cos, sin = jnp.cos(ang), jnp.sin(ang)
    cos_full = jnp.tile(jnp.concatenate([cos, cos], axis=-1), (1, N_Q_HEADS))
    sin_signed = jnp.tile(jnp.concatenate([-sin, sin], axis=-1), (1, N_Q_HEADS))
    return cos_full, sin_signed


def _angle_table(mult, n, period):
    m = (mult[:, None] * n[None, :]) % period
    th = m.astype(F32) * (2.0 * math.pi / period)
    return jnp.cos(th), jnp.sin(th)


def _dft_tables(seq):
    n = jnp.arange(seq, dtype=jnp.int32)
    t1c, t1s = _angle_table(DFT_RADIX * jnp.arange(seq // DFT_RADIX, dtype=jnp.int32), n, seq)
    t2c, t2s = _angle_table(jnp.arange(DFT_RADIX, dtype=jnp.int32), n, seq)
    return t1c, t1s, t2c, t2s


def _ssm_params(lam_re, lam_im, b_re, b_im, c_re, c_im, log_dt):
    dt = jnp.exp(log_dt)[..., None]
    mag = jnp.exp(lam_re * dt)
    ar = mag * jnp.cos(lam_im * dt)
    ai = mag * jnp.sin(lam_im * dt)
    den = lam_re * lam_re + lam_im * lam_im
    qr = ((ar - 1.0) * lam_re + ai * lam_im) / den
    qi = (ai * lam_re - (ar - 1.0) * lam_im) / den
    bb_re = qr[..., None] * b_re - qi[..., None] * b_im
    bb_im = qr[..., None] * b_im + qi[..., None] * b_re
    eye = jnp.eye(N_SSM_GROUPS, dtype=F32)

    def block_diag(x):
        y = jnp.transpose(x, (0, 1, 3, 2))[:, :, :, None, :] * eye[None, :, None, :, None]
        return y.reshape(2, x.shape[1] * x.shape[3], x.shape[1] * x.shape[2])

    bmat = jnp.concatenate([block_diag(bb_re), block_diag(bb_im)], axis=-1).astype(BF16)
    cmat = jnp.concatenate([block_diag(c_re), block_diag(-c_im)], axis=1).astype(BF16)
    a2 = jnp.stack([ar.reshape(2, N_STATE), ai.reshape(2, N_STATE)], axis=1)
    return a2, bmat, cmat


def _permute_q_heads(w_in):
    cols = jnp.arange(P_IN, dtype=jnp.int32)
    for base in (256, 768):
        blk = jnp.concatenate([base + h * HEAD_DIM + jnp.arange(HEAD_DIM, dtype=jnp.int32) for h in Q_HEAD_ORDER])
        cols = cols.at[base:base + N_Q_HEADS * HEAD_DIM].set(blk)
    return jnp.take(w_in, cols, axis=1)


def _cache_layout(cache):
    b, _, past, nkv, dh = cache.shape
    keys = cache[:, 0].reshape(b, past, nkv * dh).astype(BF16)
    vt = jnp.transpose(cache[:, 1], (0, 2, 3, 1))
    ones = jnp.ones((b, nkv, VT_ROWS - dh, past), vt.dtype)
    vt = jnp.concatenate([vt, ones], axis=2).reshape(b, nkv * VT_ROWS, past).astype(BF16)
    return keys, vt


def kernel(x_prompt, x_sample, cache_swa_kv, cache_axial_kv, state_ssm, c, c_ctx, w_mod, b_mod, norm_ffn1, norm_mix, norm_ffn2, ffn1_w_gate, ffn1_w_up, ffn1_w_down, ffn2_w_gate, ffn2_w_up, ffn2_w_down, w_in, w_out, ssm_lambda_re, ssm_lambda_im, ssm_b_re, ssm_b_im, ssm_c_re, ssm_c_im, ssm_log_dt, ssm_d, ssm_w_glu, ssm_b_glu, swa_sink, ax_q_norm, ax_k_norm, fnet_w, fnet_b, final_norm):
    depth = w_mod.shape[0]
    cb, cl, _ = x_prompt.shape
    lb, ll, _ = x_sample.shape

    cond = jnp.zeros((16, D_MODEL), F32).at[0].set(c_ctx).at[1:1 + lb].set(c)
    mod_all = _modulation(cond, w_mod, b_mod.reshape(depth, 1, N_MOD * D_MODEL))
    mod_all = mod_all.reshape(depth, 16, N_MOD, D_MODEL)

    e_heads = jnp.kron(jnp.eye(256 // HEAD_DIM, dtype=F32), jnp.ones((HEAD_DIM, HEAD_DIM), F32)) / HEAD_DIM
    e_heads = e_heads.astype(BF16)
    kc = jnp.arange(W_GRP, dtype=jnp.int32)
    cc, sc = _angle_table(kc, kc, W_GRP)
    cs_chan = jnp.concatenate([cc, sc], axis=-1).astype(BF16)
    rope_tabs = _rope_tables(ll)
    dft_ctx = _dft_tables(cl)
    dft_lat = _dft_tables(ll)
    fg = final_norm.reshape(1, D_MODEL)

    h_ctx = x_prompt.reshape(cb * cl, D_MODEL)
    h_lat = x_sample.reshape(lb * ll, D_MODEL)
    tm_lat = 512
    swa_list, ax_list, ssm_list = [], [], []
    ffn1_w = tuple(w.astype(BF16) for w in (ffn1_w_gate, ffn1_w_up, ffn1_w_down))
    ffn2_w = tuple(w.astype(BF16) for w in (ffn2_w_gate, ffn2_w_up, ffn2_w_down))
    for l in range(depth):
        mod = mod_all[l]
        bf = lambda w: w[l].astype(BF16)
        row = lambda v: v[l].reshape(1, -1)
        ffn1 = (row(norm_ffn1), *ffn1_w, fg)
        ffn2 = (row(norm_ffn2), *ffn2_w, fg)
        w_in_l, w_out_l = _permute_q_heads(w_in[l]).astype(BF16), bf(w_out)
        qn = jnp.tile(ax_q_norm[l], N_Q_HEADS).reshape(1, 256)
        kn = jnp.tile(ax_k_norm[l], KV_W // HEAD_DIM).reshape(1, KV_W)
        a2, bmat, cmat = _ssm_params(ssm_lambda_re[l], ssm_lambda_im[l], ssm_b_re[l], ssm_b_im[l],
                                     ssm_c_re[l], ssm_c_im[l], ssm_log_dt[l])
        mix_out_w = (row(ssm_d), bf(ssm_w_glu), row(ssm_b_glu), bf(fnet_w), row(fnet_b), w_out_l)
        sink = swa_sink[l]
        last = l == depth - 1

        h_ctx = _ffn(h_ctx, mod, *ffn1, layer=l, mod_base=0, tm=512, tiles_per_cond=1 << 30, cond_base=0,
                     final_norm=False)
        u_t, qkv, pc, ps, vt, kv_s, kv_a = _mix_in(h_ctx, mod, row(norm_mix), w_in_l, qn, kn, e_heads, cs_chan, None,
                                               batch=cb, seq=cl, tm=cl, cond_base=0, rope=False)
        u3 = u_t.reshape(cl, cb, W_GRP)
        zero_state = jnp.zeros((cb, 2 * N_STATE), F32)
        y_dirs, s_fins = zip(*[_ssm(u3, bmat[d], cmat[d], a2[d], zero_state, batch=cb, seq=cl, lc=128,
                                    reverse=bool(d)) for d in range(2)])
        att = _attn_ctx(sink, qkv, vt, batch=cb, seq=cl)
        f = _dft(dft_ctx, pc, ps, seq=cl, tm=cl, tk=cl, tn=2048)
        h_ctx = _mix_out(h_ctx, mod, *y_dirs, u_t, att, f, *mix_out_w,
                         batch=cb, seq=cl, tm=cl, cond_base=0, per_batch_cond=False)
        h_ctx = _ffn(h_ctx, mod, *ffn2, layer=l, mod_base=6, tm=512, tiles_per_cond=1 << 30, cond_base=0,
                     final_norm=last)
        swa_list.append(kv_s.reshape(cb, 2, cl, KV_W // HEAD_DIM, HEAD_DIM))
        ax_list.append(kv_a.reshape(cb, 2, cl, KV_W // HEAD_DIM, HEAD_DIM))
        s_fin = jnp.stack(s_fins, axis=0).reshape(2, cb, 2, N_SSM_GROUPS, SSM_STATE)
        ssm_list.append(jnp.transpose(s_fin, (1, 0, 3, 4, 2)))

        tpc = ll // tm_lat
        h_lat = _ffn(h_lat, mod, *ffn1, layer=l, mod_base=0, tm=tm_lat, tiles_per_cond=tpc, cond_base=1,
                     final_norm=False)
        u_t, qkv, pc, ps, vt = _mix_in(h_lat, mod, row(norm_mix), w_in_l, qn, kn, e_heads, cs_chan, rope_tabs,
                                       batch=lb, seq=ll, tm=tm_lat, cond_base=1, rope=True)
        s0 = jnp.transpose(state_ssm[:, l], (1, 0, 4, 2, 3)).reshape(2, lb, 2 * N_STATE)
        u3 = u_t.reshape(ll, lb, W_GRP)
        y_dirs = [_ssm(u3, bmat[d], cmat[d], a2[d], s0[d], batch=lb, seq=ll, lc=128, reverse=bool(d))[0]
                  for d in range(2)]
        kc_s, vc_s = _cache_layout(cache_swa_kv[:, l])
        kc_a, vc_a = _cache_layout(cache_axial_kv[:, l])
        att = _attn_lat(sink, qkv, vt, kc_s, vc_s, kc_a, vc_a, batch=lb, seq=ll, tq=256)
        f = _dft(dft_lat, pc, ps, seq=ll, tm=512, tk=512, tn=1024)
        h_lat = _mix_out(h_lat, mod, *y_dirs, u_t, att, f, *mix_out_w,
                         batch=lb, seq=ll, tm=tm_lat, cond_base=1, per_batch_cond=True)
        h_lat = _ffn(h_lat, mod, *ffn2, layer=l, mod_base=6, tm=tm_lat, tiles_per_cond=tpc, cond_base=1,
                     final_norm=last)

    y_prompt = h_ctx.reshape(cb, cl, D_MODEL)
    y_sample = h_lat.reshape(lb, ll, D_MODEL)
    return (y_prompt, y_sample, jnp.stack(swa_list, axis=1), jnp.stack(ax_list, axis=1),
            jnp.stack(ssm_list, axis=1))
```

```python
import functools
import math

import jax
import jax.numpy as jnp
from jax import lax
from jax.experimental import pallas as pl
from jax.experimental.pallas import tpu as pltpu

F32 = jnp.float32
BF16 = jnp.bfloat16

D_MODEL = 1024
D_FF = 2816
N_MOD = 9
EPS = 1e-6
HEAD_DIM = 64
HALF_HEAD = HEAD_DIM // 2
N_Q_HEADS = 4
Q_PER_KV = 2
W_GRP = 256
KV_W = 128
P_IN = 1536
N_SSM_GROUPS = 16
SSM_GROUP = 16
SSM_STATE = 64
N_STATE = N_SSM_GROUPS * SSM_STATE
WINDOW = 128
GRID_W = 64
ROPE_BASE = 10000.0
NEG_BIG = -1e30
LOG2E = math.log2(math.e)
Q_HEAD_ORDER = (0, 2, 1, 3)
N_KV_HEADS = N_Q_HEADS // Q_PER_KV
VT_ROWS = 80

FF_CHUNKS = (1024, 1024, 768)
SUBLANES = 8
LANES = 128
DFT_RADIX = 64
VMEM_LIMIT = 56 * 1024 * 1024


def _cparams(sem):
    return pltpu.CompilerParams(dimension_semantics=sem, vmem_limit_bytes=VMEM_LIMIT)


def _adaln(h, g, sc, sh):
    ms = jnp.mean(h * h, axis=-1, keepdims=True)
    return (h * lax.rsqrt(ms + EPS) * g) * (1.0 + sc) + sh


def _mod_kernel(c_ref, w_ref, b_ref, o_ref):
    c = c_ref[...]
    a = (c * jax.nn.sigmoid(c)).astype(BF16)
    o_ref[...] = jnp.dot(a, w_ref[...].astype(BF16), preferred_element_type=F32) + b_ref[...]


def _modulation(cond, w_mod, b_mod):
    depth = w_mod.shape[0]
    n = N_MOD * D_MODEL
    tn = 1024
    return pl.pallas_call(
        _mod_kernel,
        grid=(depth, n // tn),
        in_specs=[
            pl.BlockSpec((16, D_MODEL), lambda l, j: (0, 0)),
            pl.BlockSpec((None, D_MODEL, tn), lambda l, j: (l, 0, j)),
            pl.BlockSpec((None, 1, tn), lambda l, j: (l, 0, j)),
        ],
        out_specs=pl.BlockSpec((None, 16, tn), lambda l, j: (l, 0, j)),
        out_shape=jax.ShapeDtypeStruct((depth, 16, n), F32),
        compiler_params=_cparams(("parallel", "parallel")),
        name="modulation",
    )(cond, w_mod, b_mod)


def _ffn_kernel(h_ref, mod_ref, g_ref, wg_ref, wu_ref, wd_ref, fg_ref, o_ref, *, mod_base, final_norm):
    h = h_ref[...]
    x = _adaln(h, g_ref[...], mod_ref[0, mod_base + 1:mod_base + 2, :],
               mod_ref[0, mod_base:mod_base + 1, :]).astype(BF16)
    acc = None
    c0 = 0
    for width in FF_CHUNKS:
        gate = jnp.dot(x, wg_ref[:, c0:c0 + width], preferred_element_type=F32)
        up = jnp.dot(x, wu_ref[:, c0:c0 + width], preferred_element_type=F32)
        hg = 0.5 * gate
        a = (hg * (1.0 + jnp.tanh(hg)) * up).astype(BF16)
        part = jnp.dot(a, wd_ref[c0:c0 + width, :], preferred_element_type=F32)
        acc = part if acc is None else acc + part
        c0 += width
    hn = h + (0.5 * mod_ref[0, mod_base + 2:mod_base + 3, :]) * acc
    if final_norm:
        ms = jnp.mean(hn * hn, axis=-1, keepdims=True)
        hn = hn * lax.rsqrt(ms + EPS) * fg_ref[...]
    o_ref[...] = hn


def _ffn(h, mod, g, wg, wu, wd, fg, *, layer, mod_base, tm, tiles_per_cond, cond_base, final_norm):
    t = h.shape[0]
    cond = lambda i: (cond_base + i // tiles_per_cond, 0, 0)
    const = lambda i: (0, 0)
    of_layer = lambda i: (layer, 0, 0)
    resident = dict(pipeline_mode=pl.Buffered(1))
    kern = functools.partial(_ffn_kernel, mod_base=mod_base, final_norm=final_norm)
    return pl.pallas_call(
        kern,
        grid=(t // tm,),
        in_specs=[
            pl.BlockSpec((tm, D_MODEL), lambda i: (i, 0)),
            pl.BlockSpec((1, N_MOD, D_MODEL), cond),
            pl.BlockSpec((1, D_MODEL), const),
            pl.BlockSpec((None, D_MODEL, D_FF), of_layer, **resident),
            pl.BlockSpec((None, D_MODEL, D_FF), of_layer, **resident),
            pl.BlockSpec((None, D_FF, D_MODEL), of_layer, **resident),
            pl.BlockSpec((1, D_MODEL), const),
        ],
        out_specs=pl.BlockSpec((tm, D_MODEL), lambda i: (i, 0)),
        out_shape=jax.ShapeDtypeStruct((t, D_MODEL), F32),
        compiler_params=_cparams(("parallel",)),
        name="ffn",
    )(h, mod, g, wg, wu, wd, fg)


def _head_mean_sq(x, e):
    x2 = x * x
    hi = x2.astype(BF16)
    lo = (x2 - hi.astype(F32)).astype(BF16)
    return jnp.dot(hi, e, preferred_element_type=F32) + jnp.dot(lo, e, preferred_element_type=F32)


def _swap_halves(x):
    w = x.shape[-1]
    lane = lax.broadcasted_iota(jnp.int32, x.shape, 1)
    first = (lane & HALF_HEAD) == 0
    return jnp.where(first, pltpu.roll(x, w - HALF_HEAD, 1), pltpu.roll(x, HALF_HEAD, 1))


def _mix_in_kernel(*refs, rope):
    if rope:
        (h_ref, mod_ref, g_ref, w_ref, qn_ref, kn_ref, e_ref, cs_ref, cos_ref, sin_ref,
         u_ref, qkv_ref, pc_ref, ps_ref, vt_ref) = refs
    else:
        (h_ref, mod_ref, g_ref, w_ref, qn_ref, kn_ref, e_ref, cs_ref,
         u_ref, qkv_ref, pc_ref, ps_ref, vt_ref, kvs_ref, kva_ref) = refs
    x = _adaln(h_ref[...], g_ref[...], mod_ref[0, 4:5, :], mod_ref[0, 3:4, :]).astype(BF16)
    proj = jnp.dot(x, w_ref[...], preferred_element_type=F32)
    u_ssm = proj[:, 0:256]
    q_s = proj[:, 256:512]
    k_s = proj[:, 512:640]
    v_s = proj[:, 640:768]
    q_a = proj[:, 768:1024]
    k_a = proj[:, 1024:1152]
    v_a = proj[:, 1152:1280]
    u_f = proj[:, 1280:1536]

    e = e_ref[...]
    q_a = q_a * lax.rsqrt(_head_mean_sq(q_a, e) + EPS) * qn_ref[...]
    k_a = k_a * lax.rsqrt(_head_mean_sq(k_a, e[0:KV_W, 0:KV_W]) + EPS) * kn_ref[...]

    u_ref[...] = u_ssm
    p = jnp.dot(u_f.astype(BF16), cs_ref[...], preferred_element_type=F32)
    pc_ref[...] = p[:, 0:256].astype(BF16)
    ps_ref[...] = p[:, 256:512].astype(BF16)

    scale = LOG2E * HEAD_DIM ** -0.5
    ones = jnp.ones((VT_ROWS - HEAD_DIM, v_s.shape[0]), BF16)
    for grp, v in enumerate((v_s, v_a)):
        vt = jnp.transpose(v).astype(BF16)
        for kv in range(N_KV_HEADS):
            r0 = (grp * N_KV_HEADS + kv) * VT_ROWS
            vt_ref[r0:r0 + HEAD_DIM, :] = vt[kv * HEAD_DIM:(kv + 1) * HEAD_DIM, :]
            vt_ref[r0 + HEAD_DIM:r0 + VT_ROWS, :] = ones
    if rope:
        cos = cos_ref[...]
        sin = sin_ref[...]
        rot = lambda t, w: t * cos[:, 0:w] + _swap_halves(t) * sin[:, 0:w]
        qkv_ref[:, 0:256] = (q_s * scale).astype(BF16)
        qkv_ref[:, 256:512] = (rot(q_s, 256) * scale).astype(BF16)
        qkv_ref[:, 512:768] = (q_a * scale).astype(BF16)
        qkv_ref[:, 768:1024] = (rot(q_a, 256) * scale).astype(BF16)
        qkv_ref[:, 1024:1152] = rot(k_s, KV_W).astype(BF16)
        qkv_ref[:, 1152:1280] = rot(k_a, KV_W).astype(BF16)
    else:
        qkv_ref[:, 0:256] = (q_s * scale).astype(BF16)
        qkv_ref[:, 256:512] = (q_a * scale).astype(BF16)
        qkv_ref[:, 512:640] = k_s.astype(BF16)
        qkv_ref[:, 640:768] = k_a.astype(BF16)
        kvs_ref[0] = k_s
        kvs_ref[1] = v_s
        kva_ref[0] = k_a
        kva_ref[1] = v_a


def _mix_in(h, mod, g, w_in, qn, kn, e, cs, rope_tabs, *, batch, seq, tm, cond_base, rope):
    t = batch * seq
    tiles = seq // tm
    nc = batch * W_GRP
    cond = lambda i: (cond_base + (i // tiles if rope else 0), 0, 0)
    tcol = lambda i: (i % tiles, i // tiles)
    const = lambda i: (0, 0)
    in_specs = [
        pl.BlockSpec((tm, D_MODEL), lambda i: (i, 0)),
        pl.BlockSpec((1, N_MOD, D_MODEL), cond),
        pl.BlockSpec((1, D_MODEL), const),
        pl.BlockSpec((D_MODEL, P_IN), const),
        pl.BlockSpec((1, 256), const),
        pl.BlockSpec((1, KV_W), const),
        pl.BlockSpec((256, 256), const),
        pl.BlockSpec((256, 512), const),
    ]
    args = [h, mod, g, w_in, qn, kn, e, cs]
    qkv_w = 1280 if rope else 768
    out_specs = [
        pl.BlockSpec((tm, W_GRP), tcol),
        pl.BlockSpec((tm, qkv_w), lambda i: (i, 0)),
        pl.BlockSpec((tm, W_GRP), tcol),
        pl.BlockSpec((tm, W_GRP), tcol),
        pl.BlockSpec((None, 2 * N_KV_HEADS * VT_ROWS, tm), lambda i: (i // tiles, 0, i % tiles)),
    ]
    out_shape = [
        jax.ShapeDtypeStruct((seq, nc), F32),
        jax.ShapeDtypeStruct((t, qkv_w), BF16),
        jax.ShapeDtypeStruct((seq, nc), BF16),
        jax.ShapeDtypeStruct((seq, nc), BF16),
        jax.ShapeDtypeStruct((batch, 2 * N_KV_HEADS * VT_ROWS, seq), BF16),
    ]
    if rope:
        in_specs += [pl.BlockSpec((tm, 256), lambda i: (i % tiles, 0))] * 2
        args += list(rope_tabs)
    else:
        assert tm == seq
        out_specs += [pl.BlockSpec((None, 2, seq, KV_W), lambda i: (i, 0, 0, 0))] * 2
        out_shape += [jax.ShapeDtypeStruct((batch, 2, seq, KV_W), F32)] * 2
    return pl.pallas_call(
        functools.partial(_mix_in_kernel, rope=rope),
        grid=(t // tm,),
        in_specs=in_specs,
        out_specs=out_specs,
        out_shape=out_shape,
        compiler_params=_cparams(("parallel",)),
        name="mix_in",
    )(*args)


def _qk(a, b):
    return lax.dot_general(a, b, (((1,), (1,)), ((), ())), preferred_element_type=F32)


def _attn_ctx_kernel(sink_ref, qkv_ref, vt_ref, o_ref, *, seq):
    lane = lax.broadcasted_iota(jnp.int32, (Q_PER_KV * seq, KV_W), 1)
    for grp in range(2):
        q = jnp.concatenate([qkv_ref[:, grp * 256:grp * 256 + KV_W],
                             qkv_ref[:, grp * 256 + KV_W:(grp + 1) * 256]], axis=0)
        k = qkv_ref[:, 512 + grp * KV_W:512 + (grp + 1) * KV_W]
        outs = []
        for kv in range(N_KV_HEADS):
            keep = (lane < HEAD_DIM) if kv == 0 else (lane >= HEAD_DIM)
            s = _qk(k, jnp.where(keep, q, 0))
            m = jnp.max(s, axis=0, keepdims=True)
            if grp == 0:
                sink = jnp.concatenate(
                    [jnp.full((1, seq), sink_ref[Q_PER_KV * kv + j] * LOG2E, F32) for j in range(Q_PER_KV)], axis=1)
                m = jnp.maximum(m, sink)
            p = jnp.exp2((s - m).astype(BF16))
            r0 = (grp * N_KV_HEADS + kv) * VT_ROWS
            ot = jnp.dot(vt_ref[r0:r0 + VT_ROWS, :], p, preferred_element_type=F32)
            den = ot[HEAD_DIM:HEAD_DIM + 1, :]
            if grp == 0:
                den = den + jnp.exp2(sink - m)
            o = ot[0:HEAD_DIM, :] / den
            outs += [o[:, j * seq:(j + 1) * seq] for j in range(Q_PER_KV)]
        o_ref[:, grp * 256:(grp + 1) * 256] = jnp.transpose(jnp.concatenate(outs, axis=0)).astype(BF16)


def _attn_ctx(sink, qkv, vt, *, batch, seq):
    return pl.pallas_call(
        functools.partial(_attn_ctx_kernel, seq=seq),
        grid=(batch,),
        in_specs=[
            pl.BlockSpec(memory_space=pltpu.SMEM),
            pl.BlockSpec((seq, 768), lambda b: (b, 0)),
            pl.BlockSpec((None, 2 * N_KV_HEADS * VT_ROWS, seq), lambda b: (b, 0, 0)),
        ],
        out_specs=pl.BlockSpec((seq, 512), lambda b: (b, 0)),
        out_shape=jax.ShapeDtypeStruct((batch * seq, 512), BF16),
        compiler_params=_cparams(("parallel",)),
        name="attn_ctx",
    )(sink, qkv, vt)


def _attn_lat_kernel(sink_ref, zero_ref, q_ref, k_ref, vt_ref, kcs_ref, vcs_ref, kca_ref, vca_ref, o_ref,
                     qm_scr, m_scr, acc_scr, s0_scr, s1_scr, bm_scr, *, tq, seq, kc):
    i = pl.program_id(1)
    q0 = i * tq
    win = tq + 2 * WINDOW
    past = kcs_ref.shape[0]
    assert win <= kc and past <= kc
    ws = pl.multiple_of(jnp.clip(q0 - WINDOW, 0, seq - win), WINDOW)
    lane = lax.broadcasted_iota(jnp.int32, (Q_PER_KV * tq, KV_W), 1)

    def banded(s):
        kpos = ws + lax.broadcasted_iota(jnp.int32, s.shape, 0)
        qpos = q0 + (lax.broadcasted_iota(jnp.int32, s.shape, 1) & (tq - 1))
        return jnp.where(jnp.abs(qpos - kpos) <= WINDOW, s, NEG_BIG)

    def sink_row(kv):
        return jnp.concatenate(
            [jnp.full((1, tq), sink_ref[Q_PER_KV * kv + j] * LOG2E, F32) for j in range(Q_PER_KV)], axis=1)

    def vt_rows(grp, kv):
        r0 = (grp * N_KV_HEADS + kv) * VT_ROWS
        return slice(r0, r0 + VT_ROWS)

    acc_scr[...] = jnp.zeros(acc_scr.shape, F32)
    for grp in range(2):
        base = grp * 512
        q_plain = jnp.concatenate([q_ref[:, base:base + KV_W], q_ref[:, base + KV_W:base + 2 * KV_W]], axis=0)
        q_rope = jnp.concatenate([q_ref[:, base + 256:base + 256 + KV_W],
                                  q_ref[:, base + 256 + KV_W:base + 256 + 2 * KV_W]], axis=0)
        for kv in range(N_KV_HEADS):
            keep = (lane < HEAD_DIM) if kv == 0 else (lane >= HEAD_DIM)
            qm_scr[grp, 0, kv] = jnp.where(keep, q_plain, 0)
            qm_scr[grp, 1, kv] = jnp.where(keep, q_rope, 0)
            m_scr[grp, kv] = sink_row(kv) if grp == 0 else jnp.full((1, Q_PER_KV * tq), NEG_BIG, F32)

    jobs = [
        (0, past, lambda kv: _qk(kcs_ref[...], qm_scr[0, 0, kv]),
         lambda kv: vcs_ref[kv * VT_ROWS:(kv + 1) * VT_ROWS, :]),
        (0, win, lambda kv: banded(_qk(k_ref[pl.ds(ws, win), 0:KV_W], qm_scr[0, 1, kv])),
         lambda kv: vt_ref[vt_rows(0, kv), pl.ds(ws, win)]),
        (1, past, lambda kv: _qk(kca_ref[...], qm_scr[1, 0, kv]),
         lambda kv: vca_ref[kv * VT_ROWS:(kv + 1) * VT_ROWS, :]),
    ]
    for c in range(seq // kc):
        jobs.append((1, kc, lambda kv, c=c: _qk(k_ref[c * kc:(c + 1) * kc, KV_W:2 * KV_W], qm_scr[1, 1, kv]),
                     lambda kv, c=c: vt_ref[vt_rows(1, kv), c * kc:(c + 1) * kc]))

    s_slots = (s0_scr, s1_scr)
    row0 = pl.multiple_of(zero_ref[0], SUBLANES)

    def scores_to(slot, job):
        _, n, scores, _ = job
        for kv in range(N_KV_HEADS):
            s = scores(kv)
            s_slots[slot][kv, 0:n, :] = s
            bm_scr[slot, kv] = jnp.max(s, axis=0, keepdims=True)

    def update_from(slot, job):
        grp, n, _, values = job
        for kv in range(N_KV_HEADS):
            m_old = m_scr[grp, kv]
            m_new = jnp.maximum(m_old, bm_scr[slot, kv])
            p = jnp.exp2((s_slots[slot][kv, pl.ds(row0, n), :] - m_new).astype(BF16))
            acc_scr[grp, kv] = (jnp.exp2(m_old - m_new) * acc_scr[grp, kv]
                                + jnp.dot(values(kv), p, preferred_element_type=F32))
            m_scr[grp, kv] = m_new

    def finalize(grp):
        outs = []
        for kv in range(N_KV_HEADS):
            acc = acc_scr[grp, kv]
            den = acc[HEAD_DIM:HEAD_DIM + 1, :]
            if grp == 0:
                den = den + jnp.exp2(sink_row(kv) - m_scr[grp, kv])
            o = acc[0:HEAD_DIM, :] / den
            outs += [o[:, j * tq:(j + 1) * tq] for j in range(Q_PER_KV)]
        o_t = jnp.concatenate(outs, axis=0)
        o_ref[:, grp * 256:(grp + 1) * 256] = jnp.transpose(o_t).astype(BF16)

    scores_to(0, jobs[0])
    for j, job in enumerate(jobs):
        if j + 1 < len(jobs):
            scores_to((j + 1) % 2, jobs[j + 1])
        update_from(j % 2, job)
        if j + 1 == len(jobs) or jobs[j + 1][0] != job[0]:
            finalize(job[0])


def _attn_lat(sink, qkv, vt, kc_s, vc_s, kc_a, vc_a, *, batch, seq, tq, kc=1024):
    nq = seq // tq
    past = kc_s.shape[1]
    per_b3 = lambda b, i: (b, 0, 0)
    return pl.pallas_call(
        functools.partial(_attn_lat_kernel, tq=tq, seq=seq, kc=kc),
        grid=(batch, nq),
        in_specs=[
            pl.BlockSpec(memory_space=pltpu.SMEM),
            pl.BlockSpec(memory_space=pltpu.SMEM),
            pl.BlockSpec((tq, 1024), lambda b, i: (b * nq + i, 0)),
            pl.BlockSpec((seq, 256), lambda b, i: (b, 4)),
            pl.BlockSpec((None, 2 * N_KV_HEADS * VT_ROWS, seq), per_b3),
            pl.BlockSpec((None, past, KV_W), per_b3),
            pl.BlockSpec((None, N_KV_HEADS * VT_ROWS, past), per_b3),
            pl.BlockSpec((None, past, KV_W), per_b3),
            pl.BlockSpec((None, N_KV_HEADS * VT_ROWS, past), per_b3),
        ],
        out_specs=pl.BlockSpec((tq, 512), lambda b, i: (b * nq + i, 0)),
        out_shape=jax.ShapeDtypeStruct((batch * seq, 512), BF16),
        scratch_shapes=[
            pltpu.VMEM((2, 2, N_KV_HEADS, Q_PER_KV * tq, KV_W), BF16),
            pltpu.VMEM((2, N_KV_HEADS, 1, Q_PER_KV * tq), F32),
            pltpu.VMEM((2, N_KV_HEADS, VT_ROWS, Q_PER_KV * tq), F32),
            pltpu.VMEM((N_KV_HEADS, kc, Q_PER_KV * tq), F32),
            pltpu.VMEM((N_KV_HEADS, kc, Q_PER_KV * tq), F32),
            pltpu.VMEM((2, N_KV_HEADS, 1, Q_PER_KV * tq), F32),
        ],
        compiler_params=_cparams(("parallel", "parallel")),
        name="attn_lat",
    )(sink, jnp.zeros((1,), jnp.int32), qkv, qkv, vt, kc_s, vc_s, kc_a, vc_a)


def _ssm_kernel(u_ref, bm_ref, cm_ref, a_ref, s0_ref, y_ref, fin_ref, bu_scr, y_scr, st_scr,
                *, lc, n_sub, reverse):
    n = pl.program_id(1)

    @pl.when(n == 0)
    def _():
        st_scr[...] = s0_ref[...]

    steps = lc // n_sub
    rows = steps * SUBLANES
    ar = jnp.broadcast_to(a_ref[0:1, :], (SUBLANES, N_STATE))
    ai = jnp.broadcast_to(a_ref[1:2, :], (SUBLANES, N_STATE))

    def project_in(i):
        u = u_ref[i * steps:(i + 1) * steps].reshape(rows, W_GRP).astype(BF16)
        bu_scr[i * rows:(i + 1) * rows, :] = jnp.dot(u, bm_ref[...], preferred_element_type=F32)

    def recurrence(i, carry):
        sr, si = carry
        ts = range(steps - 1, -1, -1) if reverse else range(steps)
        for t in ts:
            r = slice(i * rows + t * SUBLANES, i * rows + (t + 1) * SUBLANES)
            nr = ar * sr - ai * si + bu_scr[r, 0:N_STATE]
            ni = ar * si + ai * sr + bu_scr[r, N_STATE:2 * N_STATE]
            bu_scr[r, 0:N_STATE] = nr
            bu_scr[r, N_STATE:2 * N_STATE] = ni
            sr, si = nr, ni
        return sr, si

    def project_out(i):
        s = bu_scr[i * rows:(i + 1) * rows, :].astype(BF16)
        y = jnp.dot(s, cm_ref[...], preferred_element_type=F32)
        for half in range(W_GRP // LANES):
            y_scr[half, i * rows:(i + 1) * rows, :] = y[:, half * LANES:(half + 1) * LANES]

    order = list(range(n_sub - 1, -1, -1) if reverse else range(n_sub))
    carry = (st_scr[:, 0:N_STATE], st_scr[:, N_STATE:2 * N_STATE])
    project_in(order[0])
    for j, i in enumerate(order):
        if j + 1 < n_sub:
            project_in(order[j + 1])
        carry = recurrence(i, carry)
        project_out(i)
    st_scr[:, 0:N_STATE] = carry[0]
    st_scr[:, N_STATE:2 * N_STATE] = carry[1]
    for b in range(SUBLANES):
        for half in range(W_GRP // LANES):
            c0 = b * W_GRP + half * LANES
            y_ref[:, c0:c0 + LANES] = y_scr[half, pl.ds(b, lc, stride=SUBLANES), :]

    @pl.when(n == pl.num_programs(1) - 1)
    def _():
        fin_ref[...] = st_scr[...]


def _ssm(u_t, bmat, cmat, a, s0, *, batch, seq, lc, reverse):
    nch = seq // lc
    chunk = (lambda n: nch - 1 - n) if reverse else (lambda n: n)
    const = lambda b, n: (0, 0)
    return pl.pallas_call(
        functools.partial(_ssm_kernel, lc=lc, n_sub=4, reverse=reverse),
        grid=(batch // SUBLANES, nch),
        in_specs=[
            pl.BlockSpec((lc, SUBLANES, W_GRP), lambda b, n: (chunk(n), b, 0)),
            pl.BlockSpec((W_GRP, 2 * N_STATE), const),
            pl.BlockSpec((2 * N_STATE, W_GRP), const),
            pl.BlockSpec((2, N_STATE), const),
            pl.BlockSpec((SUBLANES, 2 * N_STATE), lambda b, n: (b, 0)),
        ],
        out_specs=[
            pl.BlockSpec((lc, SUBLANES * W_GRP), lambda b, n: (chunk(n), b)),
            pl.BlockSpec((SUBLANES, 2 * N_STATE), lambda b, n: (b, 0)),
        ],
        out_shape=[
            jax.ShapeDtypeStruct((seq, batch * W_GRP), F32),
            jax.ShapeDtypeStruct((batch, 2 * N_STATE), F32),
        ],
        scratch_shapes=[
            pltpu.VMEM((lc * SUBLANES, 2 * N_STATE), F32),
            pltpu.VMEM((W_GRP // LANES, lc * SUBLANES, LANES), F32),
            pltpu.VMEM((SUBLANES, 2 * N_STATE), F32),
        ],
        compiler_params=_cparams(("parallel", "arbitrary")),
        name="ssm_scan",
    )(u_t, bmat, cmat, a, s0)


def _dft_kernel(t1c_ref, t1s_ref, t2c_ref, t2s_ref, pc_ref, ps_ref, o_ref, *, tm, tk, scale):
    acc = None
    for kb in range(pc_ref.shape[0] // tk):
        ks = slice(kb * tk, (kb + 1) * tk)
        t2c = t2c_ref[:, ks]
        t2s = t2s_ref[:, ks]
        cos_rows, sin_rows = [], []
        for a in range(tm // DFT_RADIX):
            c1 = t1c_ref[a:a + 1, ks]
            s1 = t1s_ref[a:a + 1, ks]
            cos_rows.append((c1 * t2c - s1 * t2s).astype(BF16))
            sin_rows.append((s1 * t2c + c1 * t2s).astype(BF16))
        part = (jnp.dot(jnp.concatenate(cos_rows, axis=0), pc_ref[ks, :], preferred_element_type=F32)
                - jnp.dot(jnp.concatenate(sin_rows, axis=0), ps_ref[ks, :], preferred_element_type=F32))
        acc = part if acc is None else acc + part
    o_ref[...] = (acc * scale).astype(BF16)


def _dft(tabs, pc, ps, *, seq, tm, tk, tn):
    nc = pc.shape[1]
    t1c, t1s, t2c, t2s = tabs
    ra = tm // DFT_RADIX
    scale = 1.0 / math.sqrt(seq * W_GRP)
    resident = dict(pipeline_mode=pl.Buffered(1))
    return pl.pallas_call(
        functools.partial(_dft_kernel, tm=tm, tk=tk, scale=scale),
        grid=(nc // tn, seq // tm),
        in_specs=[
            pl.BlockSpec((ra, seq), lambda j, i: (i, 0)),
            pl.BlockSpec((ra, seq), lambda j, i: (i, 0)),
            pl.BlockSpec((DFT_RADIX, seq), lambda j, i: (0, 0), **resident),
            pl.BlockSpec((DFT_RADIX, seq), lambda j, i: (0, 0), **resident),
            pl.BlockSpec((seq, tn), lambda j, i: (0, j), **resident),
            pl.BlockSpec((seq, tn), lambda j, i: (0, j), **resident),
        ],
        out_specs=pl.BlockSpec((tm, tn), lambda j, i: (i, j)),
        out_shape=jax.ShapeDtypeStruct((seq, nc), BF16),
        compiler_params=_cparams(("parallel", "parallel")),
        name="pos_dft",
    )(t1c, t1s, t2c, t2s, pc, ps)


def _mix_out_kernel(h_ref, mod_ref, yf_ref, yb_ref, u_ref, att_ref, f_ref, d_ref, wglu_ref, bglu_ref,
                    wf_ref, bf_ref, wo_ref, o_ref, m_scr):
    y = d_ref[...] * u_ref[...] + yf_ref[...] + yb_ref[...]
    y = y * (0.5 * (1.0 + jnp.tanh(math.sqrt(2.0 / math.pi) * (y + 0.044715 * (y * y * y)))))
    z = jnp.dot(y.astype(BF16), wglu_ref[...], preferred_element_type=F32) + bglu_ref[...]
    m_scr[:, 0:256] = (y * jax.nn.sigmoid(z)).astype(BF16)
    m_scr[:, 256:768] = att_ref[...]
    m_scr[:, 768:1024] = (jnp.dot(f_ref[...], wf_ref[...], preferred_element_type=F32) + bf_ref[...]).astype(BF16)
    mixed = jnp.dot(m_scr[...], wo_ref[...], preferred_element_type=F32)
    o_ref[...] = h_ref[...] + mod_ref[0, 5:6, :] * mixed


def _mix_out(h, mod, y_fwd, y_bwd, u_t, att, f, d, wglu, bglu, wf, bfn, wo, *, batch, seq, tm, cond_base,
             per_batch_cond):
    t = batch * seq
    tiles = seq // tm
    cond = lambda i: (cond_base + (i // tiles if per_batch_cond else 0), 0, 0)
    tcol = lambda i: (i % tiles, i // tiles)
    const = lambda i: (0, 0)
    return pl.pallas_call(
        _mix_out_kernel,
        grid=(t // tm,),
        in_specs=[
            pl.BlockSpec((tm, D_MODEL), lambda i: (i, 0)),
            pl.BlockSpec((1, N_MOD, D_MODEL), cond),
            pl.BlockSpec((tm, W_GRP), tcol),
            pl.BlockSpec((tm, W_GRP), tcol),
            pl.BlockSpec((tm, W_GRP), tcol),
            pl.BlockSpec((tm, 512), lambda i: (i, 0)),
            pl.BlockSpec((tm, W_GRP), tcol),
            pl.BlockSpec((1, W_GRP), const),
            pl.BlockSpec((W_GRP, W_GRP), const),
            pl.BlockSpec((1, W_GRP), const),
            pl.BlockSpec((W_GRP, W_GRP), const),
            pl.BlockSpec((1, W_GRP), const),
            pl.BlockSpec((D_MODEL, D_MODEL), const),
        ],
        out_specs=pl.BlockSpec((tm, D_MODEL), lambda i: (i, 0)),
        out_shape=jax.ShapeDtypeStruct((t, D_MODEL), F32),
        scratch_shapes=[pltpu.VMEM((tm, D_MODEL), BF16)],
        compiler_params=_cparams(("parallel",)),
        name="mix_out",
    )(h, mod, y_fwd, y_bwd, u_t, att, f, d, wglu, bglu, wf, bfn, wo)


def _rope_tables(seq):
    rows = seq // GRID_W
    row_id = jnp.repeat(jnp.arange(rows), GRID_W).astype(F32)
    col_id = jnp.tile(jnp.arange(GRID_W), rows).astype(F32)
    n_freq = HEAD_DIM // 4
    inv = ROPE_BASE ** (-jnp.arange(n_freq, dtype=F32) / n_freq)
    ang = jnp.concatenate([row_id[:, None] * inv, col_id[:, None] * inv], axis=-1)
    cos, sin = jnp.cos(ang), jnp.sin(ang)
    cos_full = jnp.tile(jnp.concatenate([cos, cos], axis=-1), (1, N_Q_HEADS))
    sin_signed = jnp.tile(jnp.concatenate([-sin, sin], axis=-1), (1, N_Q_HEADS))
    return cos_full, sin_signed


def _angle_table(mult, n, period):
    m = (mult[:, None] * n[None, :]) % period
    th = m.astype(F32) * (2.0 * math.pi / period)
    return jnp.cos(th), jnp.sin(th)


def _dft_tables(seq):
    n = jnp.arange(seq, dtype=jnp.int32)
    t1c, t1s = _angle_table(DFT_RADIX * jnp.arange(seq // DFT_RADIX, dtype=jnp.int32), n, seq)
    t2c, t2s = _angle_table(jnp.arange(DFT_RADIX, dtype=jnp.int32), n, seq)
    return t1c, t1s, t2c, t2s


def _ssm_params(lam_re, lam_im, b_re, b_im, c_re, c_im, log_dt):
    dt = jnp.exp(log_dt)[..., None]
    mag = jnp.exp(lam_re * dt)
    ar = mag * jnp.cos(lam_im * dt)
    ai = mag * jnp.sin(lam_im * dt)
    den = lam_re * lam_re + lam_im * lam_im
    qr = ((ar - 1.0) * lam_re + ai * lam_im) / den
    qi = (ai * lam_re - (ar - 1.0) * lam_im) / den
    bb_re = qr[..., None] * b_re - qi[..., None] * b_im
    bb_im = qr[..., None] * b_im + qi[..., None] * b_re
    eye = jnp.eye(N_SSM_GROUPS, dtype=F32)

    def block_diag(x):
        y = jnp.transpose(x, (0, 1, 3, 2))[:, :, :, None, :] * eye[None, :, None, :, None]
        return y.reshape(2, x.shape[1] * x.shape[3], x.shape[1] * x.shape[2])

    bmat = jnp.concatenate([block_diag(bb_re), block_diag(bb_im)], axis=-1).astype(BF16)
    cmat = jnp.concatenate([block_diag(c_re), block_diag(-c_im)], axis=1).astype(BF16)
    a2 = jnp.stack([ar.reshape(2, N_STATE), ai.reshape(2, N_STATE)], axis=1)
    return a2, bmat, cmat


def _permute_q_heads(w_in):
    cols = jnp.arange(P_IN, dtype=jnp.int32)
    for base in (256, 768):
        blk = jnp.concatenate([base + h * HEAD_DIM + jnp.arange(HEAD_DIM, dtype=jnp.int32) for h in Q_HEAD_ORDER])
        cols = cols.at[base:base + N_Q_HEADS * HEAD_DIM].set(blk)
    return jnp.take(w_in, cols, axis=1)


def _cache_layout(cache):
    b, _, past, nkv, dh = cache.shape
    keys = cache[:, 0].reshape(b, past, nkv * dh).astype(BF16)
    vt = jnp.transpose(cache[:, 1], (0, 2, 3, 1))
    ones = jnp.ones((b, nkv, VT_ROWS - dh, past), vt.dtype)
    vt = jnp.concatenate([vt, ones], axis=2).reshape(b, nkv * VT_ROWS, past).astype(BF16)
    return keys, vt


def kernel(x_prompt, x_sample, cache_swa_kv, cache_axial_kv, state_ssm, c, c_ctx, w_mod, b_mod, norm_ffn1, norm_mix, norm_ffn2, ffn1_w_gate, ffn1_w_up, ffn1_w_down, ffn2_w_gate, ffn2_w_up, ffn2_w_down, w_in, w_out, ssm_lambda_re, ssm_lambda_im, ssm_b_re, ssm_b_im, ssm_c_re, ssm_c_im, ssm_log_dt, ssm_d, ssm_w_glu, ssm_b_glu, swa_sink, ax_q_norm, ax_k_norm, fnet_w, fnet_b, final_norm):
    depth = w_mod.shape[0]
    cb, cl, _ = x_prompt.shape
    lb, ll, _ = x_sample.shape

    cond = jnp.zeros((16, D_MODEL), F32).at[0].set(c_ctx).at[1:1 + lb].set(c)
    mod_all = _modulation(cond, w_mod, b_mod.reshape(depth, 1, N_MOD * D_MODEL))
    mod_all = mod_all.reshape(depth, 16, N_MOD, D_MODEL)

    e_heads = jnp.kron(jnp.eye(256 // HEAD_DIM, dtype=F32), jnp.ones((HEAD_DIM, HEAD_DIM), F32)) / HEAD_DIM
    e_heads = e_heads.astype(BF16)
    kc = jnp.arange(W_GRP, dtype=jnp.int32)
    cc, sc = _angle_table(kc, kc, W_GRP)
    cs_chan = jnp.concatenate([cc, sc], axis=-1).astype(BF16)
    rope_tabs = _rope_tables(ll)
    dft_ctx = _dft_tables(cl)
    dft_lat = _dft_tables(ll)
    fg = final_norm.reshape(1, D_MODEL)

    h_ctx = x_prompt.reshape(cb * cl, D_MODEL)
    h_lat = x_sample.reshape(lb * ll, D_MODEL)
    tm_lat = 512
    swa_list, ax_list, ssm_list = [], [], []
    ffn1_w = tuple(w.astype(BF16) for w in (ffn1_w_gate, ffn1_w_up, ffn1_w_down))
    ffn2_w = tuple(w.astype(BF16) for w in (ffn2_w_gate, ffn2_w_up, ffn2_w_down))
    for l in range(depth):
        mod = mod_all[l]
        bf = lambda w: w[l].astype(BF16)
        row = lambda v: v[l].reshape(1, -1)
        ffn1 = (row(norm_ffn1), *ffn1_w, fg)
        ffn2 = (row(norm_ffn2), *ffn2_w, fg)
        w_in_l, w_out_l = _permute_q_heads(w_in[l]).astype(BF16), bf(w_out)
        qn = jnp.tile(ax_q_norm[l], N_Q_HEADS).reshape(1, 256)
        kn = jnp.tile(ax_k_norm[l], KV_W // HEAD_DIM).reshape(1, KV_W)
        a2, bmat, cmat = _ssm_params(ssm_lambda_re[l], ssm_lambda_im[l], ssm_b_re[l], ssm_b_im[l],
                                     ssm_c_re[l], ssm_c_im[l], ssm_log_dt[l])
        mix_out_w = (row(ssm_d), bf(ssm_w_glu), row(ssm_b_glu), bf(fnet_w), row(fnet_b), w_out_l)
        sink = swa_sink[l]
        last = l == depth - 1

        h_ctx = _ffn(h_ctx, mod, *ffn1, layer=l, mod_base=0, tm=512, tiles_per_cond=1 << 30, cond_base=0,
                     final_norm=False)
        u_t, qkv, pc, ps, vt, kv_s, kv_a = _mix_in(h_ctx, mod, row(norm_mix), w_in_l, qn, kn, e_heads, cs_chan, None,
                                               batch=cb, seq=cl, tm=cl, cond_base=0, rope=False)
        u3 = u_t.reshape(cl, cb, W_GRP)
        zero_state = jnp.zeros((cb, 2 * N_STATE), F32)
        y_dirs, s_fins = zip(*[_ssm(u3, bmat[d], cmat[d], a2[d], zero_state, batch=cb, seq=cl, lc=128,
                                    reverse=bool(d)) for d in range(2)])
        att = _attn_ctx(sink, qkv, vt, batch=cb, seq=cl)
        f = _dft(dft_ctx, pc, ps, seq=cl, tm=cl, tk=cl, tn=2048)
        h_ctx = _mix_out(h_ctx, mod, *y_dirs, u_t, att, f, *mix_out_w,
                         batch=cb, seq=cl, tm=cl, cond_base=0, per_batch_cond=False)
        h_ctx = _ffn(h_ctx, mod, *ffn2, layer=l, mod_base=6, tm=512, tiles_per_cond=1 << 30, cond_base=0,
                     final_norm=last)
        swa_list.append(kv_s.reshape(cb, 2, cl, KV_W // HEAD_DIM, HEAD_DIM))
        ax_list.append(kv_a.reshape(cb, 2, cl, KV_W // HEAD_DIM, HEAD_DIM))
        s_fin = jnp.stack(s_fins, axis=0).reshape(2, cb, 2, N_SSM_GROUPS, SSM_STATE)
        ssm_list.append(jnp.transpose(s_fin, (1, 0, 3, 4, 2)))

        tpc = ll // tm_lat
        h_lat = _ffn(h_lat, mod, *ffn1, layer=l, mod_base=0, tm=tm_lat, tiles_per_cond=tpc, cond_base=1,
                     final_norm=False)
        u_t, qkv, pc, ps, vt = _mix_in(h_lat, mod, row(norm_mix), w_in_l, qn, kn, e_heads, cs_chan, rope_tabs,
                                       batch=lb, seq=ll, tm=tm_lat, cond_base=1, rope=True)
        s0 = jnp.transpose(state_ssm[:, l], (1, 0, 4, 2, 3)).reshape(2, lb, 2 * N_STATE)
        u3 = u_t.reshape(ll, lb, W_GRP)
        y_dirs = [_ssm(u3, bmat[d], cmat[d], a2[d], s0[d], batch=lb, seq=ll, lc=128, reverse=bool(d))[0]
                  for d in range(2)]
        kc_s, vc_s = _cache_layout(cache_swa_kv[:, l])
        kc_a, vc_a = _cache_layout(cache_axial_kv[:, l])
        att = _attn_lat(sink, qkv, vt, kc_s, vc_s, kc_a, vc_a, batch=lb, seq=ll, tq=256)
        f = _dft(dft_lat, pc, ps, seq=ll, tm=512, tk=2048, tn=1024)
        h_lat = _mix_out(h_lat, mod, *y_dirs, u_t, att, f, *mix_out_w,
                         batch=lb, seq=ll, tm=tm_lat, cond_base=1, per_batch_cond=True)
        h_lat = _ffn(h_lat, mod, *ffn2, layer=l, mod_base=6, tm=tm_lat, tiles_per_cond=tpc, cond_base=1,
                     final_norm=last)

    y_prompt = h_ctx.reshape(cb, cl, D_MODEL)
    y_sample = h_lat.reshape(lb, ll, D_MODEL)
    return (y_prompt, y_sample, jnp.stack(swa_list, axis=1), jnp.stack(ax_list, axis=1),
            jnp.stack(ssm_list, axis=1))
```

```python
import functools
import math

import jax
import jax.numpy as jnp
from jax import lax
from jax.experimental import pallas as pl
from jax.experimental.pallas import tpu as pltpu

F32 = jnp.float32
BF16 = jnp.bfloat16

D_MODEL = 1024
D_FF = 2816
N_MOD = 9
EPS = 1e-6
HEAD_DIM = 64
HALF_HEAD = HEAD_DIM // 2
N_Q_HEADS = 4
Q_PER_KV = 2
W_GRP = 256
KV_W = 128
P_IN = 1536
N_SSM_GROUPS = 16
SSM_GROUP = 16
SSM_STATE = 64
N_STATE = N_SSM_GROUPS * SSM_STATE
WINDOW = 128
GRID_W = 64
ROPE_BASE = 10000.0
NEG_BIG = -1e30
LOG2E = math.log2(math.e)
Q_HEAD_ORDER = (0, 2, 1, 3)
N_KV_HEADS = N_Q_HEADS // Q_PER_KV
VT_ROWS = 80

FF_CHUNKS = (1024, 1024, 768)
SUBLANES = 8
LANES = 128
DFT_RADIX = 64
VMEM_LIMIT = 56 * 1024 * 1024


def _cparams(sem):
    return pltpu.CompilerParams(dimension_semantics=sem, vmem_limit_bytes=VMEM_LIMIT)


def _adaln(h, g, sc, sh):
    ms = jnp.mean(h * h, axis=-1, keepdims=True)
    return (h * lax.rsqrt(ms + EPS) * g) * (1.0 + sc) + sh


def _mod_kernel(c_ref, w_ref, b_ref, o_ref):
    c = c_ref[...]
    a = (c * jax.nn.sigmoid(c)).astype(BF16)
    o_ref[...] = jnp.dot(a, w_ref[...].astype(BF16), preferred_element_type=F32) + b_ref[...]


def _modulation(cond, w_mod, b_mod):
    depth = w_mod.shape[0]
    n = N_MOD * D_MODEL
    tn = 1024
    return pl.pallas_call(
        _mod_kernel,
        grid=(depth, n // tn),
        in_specs=[
            pl.BlockSpec((16, D_MODEL), lambda l, j: (0, 0)),
            pl.BlockSpec((None, D_MODEL, tn), lambda l, j: (l, 0, j)),
            pl.BlockSpec((None, 1, tn), lambda l, j: (l, 0, j)),
        ],
        out_specs=pl.BlockSpec((None, 16, tn), lambda l, j: (l, 0, j)),
        out_shape=jax.ShapeDtypeStruct((depth, 16, n), F32),
        compiler_params=_cparams(("parallel", "parallel")),
        name="modulation",
    )(cond, w_mod, b_mod)


def _ffn_body(h, mod_ref, g_ref, wg_ref, wu_ref, wd_ref, fg_ref, *, mod_base, final_norm):
    x = _adaln(h, g_ref[...], mod_ref[0, mod_base + 1:mod_base + 2, :],
               mod_ref[0, mod_base:mod_base + 1, :]).astype(BF16)
    acc = None
    c0 = 0
    for width in FF_CHUNKS:
        gate = jnp.dot(x, wg_ref[:, c0:c0 + width], preferred_element_type=F32)
        up = jnp.dot(x, wu_ref[:, c0:c0 + width], preferred_element_type=F32)
        hg = 0.5 * gate
        a = (hg * (1.0 + jnp.tanh(hg)) * up).astype(BF16)
        part = jnp.dot(a, wd_ref[c0:c0 + width, :], preferred_element_type=F32)
        acc = part if acc is None else acc + part
        c0 += width
    hn = h + (0.5 * mod_ref[0, mod_base + 2:mod_base + 3, :]) * acc
    if final_norm:
        ms = jnp.mean(hn * hn, axis=-1, keepdims=True)
        hn = hn * lax.rsqrt(ms + EPS) * fg_ref[...]
    return hn


def _ffn_kernel(h_ref, mod_ref, g_ref, wg_ref, wu_ref, wd_ref, fg_ref, o_ref, *, mod_base, final_norm):
    o_ref[...] = _ffn_body(h_ref[...], mod_ref, g_ref, wg_ref, wu_ref, wd_ref, fg_ref,
                           mod_base=mod_base, final_norm=final_norm)


def _ffn(h, mod, g, wg, wu, wd, fg, *, layer, mod_base, tm, tiles_per_cond, cond_base, final_norm):
    t = h.shape[0]
    cond = lambda i: (cond_base + i // tiles_per_cond, 0, 0)
    const = lambda i: (0, 0)
    of_layer = lambda i: (layer, 0, 0)
    resident = dict(pipeline_mode=pl.Buffered(1))
    kern = functools.partial(_ffn_kernel, mod_base=mod_base, final_norm=final_norm)
    return pl.pallas_call(
        kern,
        grid=(t // tm,),
        in_specs=[
            pl.BlockSpec((tm, D_MODEL), lambda i: (i, 0)),
            pl.BlockSpec((1, N_MOD, D_MODEL), cond),
            pl.BlockSpec((1, D_MODEL), const),
            pl.BlockSpec((None, D_MODEL, D_FF), of_layer, **resident),
            pl.BlockSpec((None, D_MODEL, D_FF), of_layer, **resident),
            pl.BlockSpec((None, D_FF, D_MODEL), of_layer, **resident),
            pl.BlockSpec((1, D_MODEL), const),
        ],
        out_specs=pl.BlockSpec((tm, D_MODEL), lambda i: (i, 0)),
        out_shape=jax.ShapeDtypeStruct((t, D_MODEL), F32),
        compiler_params=_cparams(("parallel",)),
        name="ffn",
    )(h, mod, g, wg, wu, wd, fg)


def _head_mean_sq(x, e):
    x2 = x * x
    hi = x2.astype(BF16)
    lo = (x2 - hi.astype(F32)).astype(BF16)
    return jnp.dot(hi, e, preferred_element_type=F32) + jnp.dot(lo, e, preferred_element_type=F32)


def _swap_halves(x):
    w = x.shape[-1]
    lane = lax.broadcasted_iota(jnp.int32, x.shape, 1)
    first = (lane & HALF_HEAD) == 0
    return jnp.where(first, pltpu.roll(x, w - HALF_HEAD, 1), pltpu.roll(x, HALF_HEAD, 1))


def _mix_in_kernel(*refs, rope):
    if rope:
        (h_ref, mod_ref, g_ref, w_ref, qn_ref, kn_ref, e_ref, cs_ref, cos_ref, sin_ref,
         u_ref, qkv_ref, pc_ref, ps_ref, vt_ref) = refs
    else:
        (h_ref, mod_ref, g_ref, w_ref, qn_ref, kn_ref, e_ref, cs_ref,
         u_ref, qkv_ref, pc_ref, ps_ref, vt_ref, kvs_ref, kva_ref) = refs
    x = _adaln(h_ref[...], g_ref[...], mod_ref[0, 4:5, :], mod_ref[0, 3:4, :]).astype(BF16)
    proj = jnp.dot(x, w_ref[...], preferred_element_type=F32)
    u_ssm = proj[:, 0:256]
    q_s = proj[:, 256:512]
    k_s = proj[:, 512:640]
    v_s = proj[:, 640:768]
    q_a = proj[:, 768:1024]
    k_a = proj[:, 1024:1152]
    v_a = proj[:, 1152:1280]
    u_f = proj[:, 1280:1536]

    e = e_ref[...]
    q_a = q_a * lax.rsqrt(_head_mean_sq(q_a, e) + EPS) * qn_ref[...]
    k_a = k_a * lax.rsqrt(_head_mean_sq(k_a, e[0:KV_W, 0:KV_W]) + EPS) * kn_ref[...]

    u_ref[...] = u_ssm
    p = jnp.dot(u_f.astype(BF16), cs_ref[...], preferred_element_type=F32)
    pc_ref[...] = p[:, 0:256].astype(BF16)
    ps_ref[...] = p[:, 256:512].astype(BF16)

    scale = LOG2E * HEAD_DIM ** -0.5
    ones = jnp.ones((VT_ROWS - HEAD_DIM, v_s.shape[0]), BF16)
    for grp, v in enumerate((v_s, v_a)):
        vt = jnp.transpose(v).astype(BF16)
        for kv in range(N_KV_HEADS):
            r0 = (grp * N_KV_HEADS + kv) * VT_ROWS
            vt_ref[r0:r0 + HEAD_DIM, :] = vt[kv * HEAD_DIM:(kv + 1) * HEAD_DIM, :]
            vt_ref[r0 + HEAD_DIM:r0 + VT_ROWS, :] = ones
    if rope:
        cos = cos_ref[...]
        sin = sin_ref[...]
        rot = lambda t, w: t * cos[:, 0:w] + _swap_halves(t) * sin[:, 0:w]
        qkv_ref[:, 0:256] = (q_s * scale).astype(BF16)
        qkv_ref[:, 256:512] = (rot(q_s, 256) * scale).astype(BF16)
        qkv_ref[:, 512:768] = (q_a * scale).astype(BF16)
        qkv_ref[:, 768:1024] = (rot(q_a, 256) * scale).astype(BF16)
        qkv_ref[:, 1024:1152] = rot(k_s, KV_W).astype(BF16)
        qkv_ref[:, 1152:1280] = rot(k_a, KV_W).astype(BF16)
    else:
        qkv_ref[:, 0:256] = (q_s * scale).astype(BF16)
        qkv_ref[:, 256:512] = (q_a * scale).astype(BF16)
        qkv_ref[:, 512:640] = k_s.astype(BF16)
        qkv_ref[:, 640:768] = k_a.astype(BF16)
        kvs_ref[0] = k_s
        kvs_ref[1] = v_s
        kva_ref[0] = k_a
        kva_ref[1] = v_a


def _mix_in(h, mod, g, w_in, qn, kn, e, cs, rope_tabs, *, batch, seq, tm, cond_base, rope):
    t = batch * seq
    tiles = seq // tm
    nc = batch * W_GRP
    cond = lambda i: (cond_base + (i // tiles if rope else 0), 0, 0)
    tcol = lambda i: (i % tiles, i // tiles)
    const = lambda i: (0, 0)
    in_specs = [
        pl.BlockSpec((tm, D_MODEL), lambda i: (i, 0)),
        pl.BlockSpec((1, N_MOD, D_MODEL), cond),
        pl.BlockSpec((1, D_MODEL), const),
        pl.BlockSpec((D_MODEL, P_IN), const),
        pl.BlockSpec((1, 256), const),
        pl.BlockSpec((1, KV_W), const),
        pl.BlockSpec((256, 256), const),
        pl.BlockSpec((256, 512), const),
    ]
    args = [h, mod, g, w_in, qn, kn, e, cs]
    qkv_w = 1280 if rope else 768
    out_specs = [
        pl.BlockSpec((tm, W_GRP), tcol),
        pl.BlockSpec((tm, qkv_w), lambda i: (i, 0)),
        pl.BlockSpec((tm, W_GRP), tcol),
        pl.BlockSpec((tm, W_GRP), tcol),
        pl.BlockSpec((None, 2 * N_KV_HEADS * VT_ROWS, tm), lambda i: (i // tiles, 0, i % tiles)),
    ]
    out_shape = [
        jax.ShapeDtypeStruct((seq, nc), F32),
        jax.ShapeDtypeStruct((t, qkv_w), BF16),
        jax.ShapeDtypeStruct((seq, nc), BF16),
        jax.ShapeDtypeStruct((seq, nc), BF16),
        jax.ShapeDtypeStruct((batch, 2 * N_KV_HEADS * VT_ROWS, seq), BF16),
    ]
    if rope:
        in_specs += [pl.BlockSpec((tm, 256), lambda i: (i % tiles, 0))] * 2
        args += list(rope_tabs)
    else:
        assert tm == seq
        out_specs += [pl.BlockSpec((None, 2, seq, KV_W), lambda i: (i, 0, 0, 0))] * 2
        out_shape += [jax.ShapeDtypeStruct((batch, 2, seq, KV_W), F32)] * 2
    return pl.pallas_call(
        functools.partial(_mix_in_kernel, rope=rope),
        grid=(t // tm,),
        in_specs=in_specs,
        out_specs=out_specs,
        out_shape=out_shape,
        compiler_params=_cparams(("parallel",)),
        name="mix_in",
    )(*args)


def _qk(a, b):
    return lax.dot_general(a, b, (((1,), (1,)), ((), ())), preferred_element_type=F32)


def _attn_ctx_kernel(sink_ref, qkv_ref, vt_ref, o_ref, *, seq):
    lane = lax.broadcasted_iota(jnp.int32, (Q_PER_KV * seq, KV_W), 1)
    for grp in range(2):
        q = jnp.concatenate([qkv_ref[:, grp * 256:grp * 256 + KV_W],
                             qkv_ref[:, grp * 256 + KV_W:(grp + 1) * 256]], axis=0)
        k = qkv_ref[:, 512 + grp * KV_W:512 + (grp + 1) * KV_W]
        outs = []
        for kv in range(N_KV_HEADS):
            keep = (lane < HEAD_DIM) if kv == 0 else (lane >= HEAD_DIM)
            s = _qk(k, jnp.where(keep, q, 0))
            m = jnp.max(s, axis=0, keepdims=True)
            if grp == 0:
                sink = jnp.concatenate(
                    [jnp.full((1, seq), sink_ref[Q_PER_KV * kv + j] * LOG2E, F32) for j in range(Q_PER_KV)], axis=1)
                m = jnp.maximum(m, sink)
            p = jnp.exp2((s - m).astype(BF16))
            r0 = (grp * N_KV_HEADS + kv) * VT_ROWS
            ot = jnp.dot(vt_ref[r0:r0 + VT_ROWS, :], p, preferred_element_type=F32)
            den = ot[HEAD_DIM:HEAD_DIM + 1, :]
            if grp == 0:
                den = den + jnp.exp2(sink - m)
            o = ot[0:HEAD_DIM, :] / den
            outs += [o[:, j * seq:(j + 1) * seq] for j in range(Q_PER_KV)]
        o_ref[:, grp * 256:(grp + 1) * 256] = jnp.transpose(jnp.concatenate(outs, axis=0)).astype(BF16)


def _attn_ctx(sink, qkv, vt, *, batch, seq):
    return pl.pallas_call(
        functools.partial(_attn_ctx_kernel, seq=seq),
        grid=(batch,),
        in_specs=[
            pl.BlockSpec(memory_space=pltpu.SMEM),
            pl.BlockSpec((seq, 768), lambda b: (b, 0)),
            pl.BlockSpec((None, 2 * N_KV_HEADS * VT_ROWS, seq), lambda b: (b, 0, 0)),
        ],
        out_specs=pl.BlockSpec((seq, 512), lambda b: (b, 0)),
        out_shape=jax.ShapeDtypeStruct((batch * seq, 512), BF16),
        compiler_params=_cparams(("parallel",)),
        name="attn_ctx",
    )(sink, qkv, vt)


def _attn_lat_kernel(sink_ref, zero_ref, q_ref, k_ref, vt_ref, kcs_ref, vcs_ref, kca_ref, vca_ref, o_ref,
                     qm_scr, m_scr, acc_scr, s0_scr, s1_scr, bm_scr, *, tq, seq, kc):
    i = pl.program_id(1)
    q0 = i * tq
    win = tq + 2 * WINDOW
    past = kcs_ref.shape[0]
    assert win <= kc and past <= kc
    ws = pl.multiple_of(jnp.clip(q0 - WINDOW, 0, seq - win), WINDOW)
    lane = lax.broadcasted_iota(jnp.int32, (Q_PER_KV * tq, KV_W), 1)

    def banded(s):
        kpos = ws + lax.broadcasted_iota(jnp.int32, s.shape, 0)
        qpos = q0 + (lax.broadcasted_iota(jnp.int32, s.shape, 1) & (tq - 1))
        return jnp.where(jnp.abs(qpos - kpos) <= WINDOW, s, NEG_BIG)

    def sink_row(kv):
        return jnp.concatenate(
            [jnp.full((1, tq), sink_ref[Q_PER_KV * kv + j] * LOG2E, F32) for j in range(Q_PER_KV)], axis=1)

    def vt_rows(grp, kv):
        r0 = (grp * N_KV_HEADS + kv) * VT_ROWS
        return slice(r0, r0 + VT_ROWS)

    acc_scr[...] = jnp.zeros(acc_scr.shape, F32)
    for grp in range(2):
        base = grp * 512
        q_plain = jnp.concatenate([q_ref[:, base:base + KV_W], q_ref[:, base + KV_W:base + 2 * KV_W]], axis=0)
        q_rope = jnp.concatenate([q_ref[:, base + 256:base + 256 + KV_W],
                                  q_ref[:, base + 256 + KV_W:base + 256 + 2 * KV_W]], axis=0)
        for kv in range(N_KV_HEADS):
            keep = (lane < HEAD_DIM) if kv == 0 else (lane >= HEAD_DIM)
            qm_scr[grp, 0, kv] = jnp.where(keep, q_plain, 0)
            qm_scr[grp, 1, kv] = jnp.where(keep, q_rope, 0)
            m_scr[grp, kv] = sink_row(kv) if grp == 0 else jnp.full((1, Q_PER_KV * tq), NEG_BIG, F32)

    jobs = [
        (0, past, lambda kv: _qk(kcs_ref[...], qm_scr[0, 0, kv]),
         lambda kv: vcs_ref[kv * VT_ROWS:(kv + 1) * VT_ROWS, :]),
        (0, win, lambda kv: banded(_qk(k_ref[pl.ds(ws, win), 0:KV_W], qm_scr[0, 1, kv])),
         lambda kv: vt_ref[vt_rows(0, kv), pl.ds(ws, win)]),
        (1, past, lambda kv: _qk(kca_ref[...], qm_scr[1, 0, kv]),
         lambda kv: vca_ref[kv * VT_ROWS:(kv + 1) * VT_ROWS, :]),
    ]
    for c in range(seq // kc):
        jobs.append((1, kc, lambda kv, c=c: _qk(k_ref[c * kc:(c + 1) * kc, KV_W:2 * KV_W], qm_scr[1, 1, kv]),
                     lambda kv, c=c: vt_ref[vt_rows(1, kv), c * kc:(c + 1) * kc]))

    s_slots = (s0_scr, s1_scr)
    row0 = pl.multiple_of(zero_ref[0], SUBLANES)

    def scores_to(slot, job):
        _, n, scores, _ = job
        for kv in range(N_KV_HEADS):
            s = scores(kv)
            s_slots[slot][kv, 0:n, :] = s
            bm_scr[slot, kv] = jnp.max(s, axis=0, keepdims=True)

    def update_from(slot, job):
        grp, n, _, values = job
        for kv in range(N_KV_HEADS):
            m_old = m_scr[grp, kv]
            m_new = jnp.maximum(m_old, bm_scr[slot, kv])
            p = jnp.exp2((s_slots[slot][kv, pl.ds(row0, n), :] - m_new).astype(BF16))
            acc_scr[grp, kv] = (jnp.exp2(m_old - m_new) * acc_scr[grp, kv]
                                + jnp.dot(values(kv), p, preferred_element_type=F32))
            m_scr[grp, kv] = m_new

    def finalize(grp):
        outs = []
        for kv in range(N_KV_HEADS):
            acc = acc_scr[grp, kv]
            den = acc[HEAD_DIM:HEAD_DIM + 1, :]
            if grp == 0:
                den = den + jnp.exp2(sink_row(kv) - m_scr[grp, kv])
            o = acc[0:HEAD_DIM, :] / den
            outs += [o[:, j * tq:(j + 1) * tq] for j in range(Q_PER_KV)]
        o_t = jnp.concatenate(outs, axis=0)
        o_ref[:, grp * 256:(grp + 1) * 256] = jnp.transpose(o_t).astype(BF16)

    scores_to(0, jobs[0])
    for j, job in enumerate(jobs):
        if j + 1 < len(jobs):
            scores_to((j + 1) % 2, jobs[j + 1])
        update_from(j % 2, job)
        if j + 1 == len(jobs) or jobs[j + 1][0] != job[0]:
            finalize(job[0])


def _attn_lat(sink, qkv, vt, kc_s, vc_s, kc_a, vc_a, *, batch, seq, tq, kc=1024):
    nq = seq // tq
    past = kc_s.shape[1]
    per_b3 = lambda b, i: (b, 0, 0)
    return pl.pallas_call(
        functools.partial(_attn_lat_kernel, tq=tq, seq=seq, kc=kc),
        grid=(batch, nq),
        in_specs=[
            pl.BlockSpec(memory_space=pltpu.SMEM),
            pl.BlockSpec(memory_space=pltpu.SMEM),
            pl.BlockSpec((tq, 1024), lambda b, i: (b * nq + i, 0)),
            pl.BlockSpec((seq, 256), lambda b, i: (b, 4)),
            pl.BlockSpec((None, 2 * N_KV_HEADS * VT_ROWS, seq), per_b3),
            pl.BlockSpec((None, past, KV_W), per_b3),
            pl.BlockSpec((None, N_KV_HEADS * VT_ROWS, past), per_b3),
            pl.BlockSpec((None, past, KV_W), per_b3),
            pl.BlockSpec((None, N_KV_HEADS * VT_ROWS, past), per_b3),
        ],
        out_specs=pl.BlockSpec((tq, 512), lambda b, i: (b * nq + i, 0)),
        out_shape=jax.ShapeDtypeStruct((batch * seq, 512), BF16),
        scratch_shapes=[
            pltpu.VMEM((2, 2, N_KV_HEADS, Q_PER_KV * tq, KV_W), BF16),
            pltpu.VMEM((2, N_KV_HEADS, 1, Q_PER_KV * tq), F32),
            pltpu.VMEM((2, N_KV_HEADS, VT_ROWS, Q_PER_KV * tq), F32),
            pltpu.VMEM((N_KV_HEADS, kc, Q_PER_KV * tq), F32),
            pltpu.VMEM((N_KV_HEADS, kc, Q_PER_KV * tq), F32),
            pltpu.VMEM((2, N_KV_HEADS, 1, Q_PER_KV * tq), F32),
        ],
        compiler_params=_cparams(("parallel", "parallel")),
        name="attn_lat",
    )(sink, jnp.zeros((1,), jnp.int32), qkv, qkv, vt, kc_s, vc_s, kc_a, vc_a)


def _ssm_kernel(u_ref, bm_ref, cm_ref, a_ref, s0_ref, y_ref, fin_ref, bu_scr, y_scr, st_scr,
                *, lc, n_sub, reverse):
    n = pl.program_id(1)

    @pl.when(n == 0)
    def _():
        st_scr[...] = s0_ref[...]

    steps = lc // n_sub
    rows = steps * SUBLANES
    ar = jnp.broadcast_to(a_ref[0:1, :], (SUBLANES, N_STATE))
    ai = jnp.broadcast_to(a_ref[1:2, :], (SUBLANES, N_STATE))

    def project_in(i):
        u = u_ref[i * steps:(i + 1) * steps].reshape(rows, W_GRP).astype(BF16)
        bu_scr[i * rows:(i + 1) * rows, :] = jnp.dot(u, bm_ref[...], preferred_element_type=F32)

    def recurrence(i, carry):
        sr, si = carry
        ts = range(steps - 1, -1, -1) if reverse else range(steps)
        for t in ts:
            r = slice(i * rows + t * SUBLANES, i * rows + (t + 1) * SUBLANES)
            nr = ar * sr - ai * si + bu_scr[r, 0:N_STATE]
            ni = ar * si + ai * sr + bu_scr[r, N_STATE:2 * N_STATE]
            bu_scr[r, 0:N_STATE] = nr
            bu_scr[r, N_STATE:2 * N_STATE] = ni
            sr, si = nr, ni
        return sr, si

    def project_out(i):
        s = bu_scr[i * rows:(i + 1) * rows, :].astype(BF16)
        y = jnp.dot(s, cm_ref[...], preferred_element_type=F32)
        for half in range(W_GRP // LANES):
            y_scr[half, i * rows:(i + 1) * rows, :] = y[:, half * LANES:(half + 1) * LANES]

    order = list(range(n_sub - 1, -1, -1) if reverse else range(n_sub))
    carry = (st_scr[:, 0:N_STATE], st_scr[:, N_STATE:2 * N_STATE])
    project_in(order[0])
    for j, i in enumerate(order):
        if j + 1 < n_sub:
            project_in(order[j + 1])
        carry = recurrence(i, carry)
        project_out(i)
    st_scr[:, 0:N_STATE] = carry[0]
    st_scr[:, N_STATE:2 * N_STATE] = carry[1]
    for b in range(SUBLANES):
        for half in range(W_GRP // LANES):
            c0 = b * W_GRP + half * LANES
            y_ref[:, c0:c0 + LANES] = y_scr[half, pl.ds(b, lc, stride=SUBLANES), :]

    @pl.when(n == pl.num_programs(1) - 1)
    def _():
        fin_ref[...] = st_scr[...]


def _ssm(u_t, bmat, cmat, a, s0, *, batch, seq, lc, reverse):
    nch = seq // lc
    chunk = (lambda n: nch - 1 - n) if reverse else (lambda n: n)
    const = lambda b, n: (0, 0)
    return pl.pallas_call(
        functools.partial(_ssm_kernel, lc=lc, n_sub=4, reverse=reverse),
        grid=(batch // SUBLANES, nch),
        in_specs=[
            pl.BlockSpec((lc, SUBLANES, W_GRP), lambda b, n: (chunk(n), b, 0)),
            pl.BlockSpec((W_GRP, 2 * N_STATE), const),
            pl.BlockSpec((2 * N_STATE, W_GRP), const),
            pl.BlockSpec((2, N_STATE), const),
            pl.BlockSpec((SUBLANES, 2 * N_STATE), lambda b, n: (b, 0)),
        ],
        out_specs=[
            pl.BlockSpec((lc, SUBLANES * W_GRP), lambda b, n: (chunk(n), b)),
            pl.BlockSpec((SUBLANES, 2 * N_STATE), lambda b, n: (b, 0)),
        ],
        out_shape=[
            jax.ShapeDtypeStruct((seq, batch * W_GRP), F32),
            jax.ShapeDtypeStruct((batch, 2 * N_STATE), F32),
        ],
        scratch_shapes=[
            pltpu.VMEM((lc * SUBLANES, 2 * N_STATE), F32),
            pltpu.VMEM((W_GRP // LANES, lc * SUBLANES, LANES), F32),
            pltpu.VMEM((SUBLANES, 2 * N_STATE), F32),
        ],
        compiler_params=_cparams(("parallel", "arbitrary")),
        name="ssm_scan",
    )(u_t, bmat, cmat, a, s0)


def _dft_kernel(t1c_ref, t1s_ref, t2c_ref, t2s_ref, pc_ref, ps_ref, o_ref, *, tm, tk, scale):
    acc = None
    for kb in range(pc_ref.shape[0] // tk):
        ks = slice(kb * tk, (kb + 1) * tk)
        t2c = t2c_ref[:, ks]
        t2s = t2s_ref[:, ks]
        cos_rows, sin_rows = [], []
        for a in range(tm // DFT_RADIX):
            c1 = t1c_ref[a:a + 1, ks]
            s1 = t1s_ref[a:a + 1, ks]
            cos_rows.append((c1 * t2c - s1 * t2s).astype(BF16))
            sin_rows.append((s1 * t2c + c1 * t2s).astype(BF16))
        part = (jnp.dot(jnp.concatenate(cos_rows, axis=0), pc_ref[ks, :], preferred_element_type=F32)
                - jnp.dot(jnp.concatenate(sin_rows, axis=0), ps_ref[ks, :], preferred_element_type=F32))
        acc = part if acc is None else acc + part
    o_ref[...] = (acc * scale).astype(BF16)


def _dft(tabs, pc, ps, *, seq, tm, tk, tn):
    nc = pc.shape[1]
    t1c, t1s, t2c, t2s = tabs
    ra = tm // DFT_RADIX
    scale = 1.0 / math.sqrt(seq * W_GRP)
    resident = dict(pipeline_mode=pl.Buffered(1))
    return pl.pallas_call(
        functools.partial(_dft_kernel, tm=tm, tk=tk, scale=scale),
        grid=(nc // tn, seq // tm),
        in_specs=[
            pl.BlockSpec((ra, seq), lambda j, i: (i, 0)),
            pl.BlockSpec((ra, seq), lambda j, i: (i, 0)),
            pl.BlockSpec((DFT_RADIX, seq), lambda j, i: (0, 0), **resident),
            pl.BlockSpec((DFT_RADIX, seq), lambda j, i: (0, 0), **resident),
            pl.BlockSpec((seq, tn), lambda j, i: (0, j), **resident),
            pl.BlockSpec((seq, tn), lambda j, i: (0, j), **resident),
        ],
        out_specs=pl.BlockSpec((tm, tn), lambda j, i: (i, j)),
        out_shape=jax.ShapeDtypeStruct((seq, nc), BF16),
        compiler_params=_cparams(("parallel", "parallel")),
        name="pos_dft",
    )(t1c, t1s, t2c, t2s, pc, ps)


def _mix_ffn_kernel(h_ref, mod_ref, yf_ref, yb_ref, u_ref, att_ref, f_ref, d_ref, wglu_ref, bglu_ref,
                    wf_ref, bf_ref, wo_ref, g_ref, wg_ref, wu_ref, wd_ref, fg_ref, o_ref, m_scr, *, final_norm):
    y = d_ref[...] * u_ref[...] + yf_ref[...] + yb_ref[...]
    y = y * (0.5 * (1.0 + jnp.tanh(math.sqrt(2.0 / math.pi) * (y + 0.044715 * (y * y * y)))))
    z = jnp.dot(y.astype(BF16), wglu_ref[...], preferred_element_type=F32) + bglu_ref[...]
    m_scr[:, 0:256] = (y * jax.nn.sigmoid(z)).astype(BF16)
    m_scr[:, 256:768] = att_ref[...]
    m_scr[:, 768:1024] = (jnp.dot(f_ref[...], wf_ref[...], preferred_element_type=F32) + bf_ref[...]).astype(BF16)
    mixed = jnp.dot(m_scr[...], wo_ref[...], preferred_element_type=F32)
    h_mid = h_ref[...] + mod_ref[0, 5:6, :] * mixed
    o_ref[...] = _ffn_body(h_mid, mod_ref, g_ref, wg_ref, wu_ref, wd_ref, fg_ref, mod_base=6, final_norm=final_norm)


def _mix_ffn(h, mod, y_fwd, y_bwd, u_t, att, f, d, wglu, bglu, wf, bfn, wo, g, wg, wu, wd, fg, *, layer, batch, seq,
             tm, cond_base, per_batch_cond, final_norm):
    t = batch * seq
    tiles = seq // tm
    cond = lambda i: (cond_base + (i // tiles if per_batch_cond else 0), 0, 0)
    tcol = lambda i: (i % tiles, i // tiles)
    const = lambda i: (0, 0)
    of_layer = lambda i: (layer, 0, 0)
    resident = dict(pipeline_mode=pl.Buffered(1))
    return pl.pallas_call(
        functools.partial(_mix_ffn_kernel, final_norm=final_norm),
        grid=(t // tm,),
        in_specs=[
            pl.BlockSpec((tm, D_MODEL), lambda i: (i, 0)),
            pl.BlockSpec((1, N_MOD, D_MODEL), cond),
            pl.BlockSpec((tm, W_GRP), tcol),
            pl.BlockSpec((tm, W_GRP), tcol),
            pl.BlockSpec((tm, W_GRP), tcol),
            pl.BlockSpec((tm, 512), lambda i: (i, 0)),
            pl.BlockSpec((tm, W_GRP), tcol),
            pl.BlockSpec((1, W_GRP), const),
            pl.BlockSpec((W_GRP, W_GRP), const),
            pl.BlockSpec((1, W_GRP), const),
            pl.BlockSpec((W_GRP, W_GRP), const),
            pl.BlockSpec((1, W_GRP), const),
            pl.BlockSpec((D_MODEL, D_MODEL), const, **resident),
            pl.BlockSpec((1, D_MODEL), const),
            pl.BlockSpec((None, D_MODEL, D_FF), of_layer, **resident),
            pl.BlockSpec((None, D_MODEL, D_FF), of_layer, **resident),
            pl.BlockSpec((None, D_FF, D_MODEL), of_layer, **resident),
            pl.BlockSpec((1, D_MODEL), const),
        ],
        out_specs=pl.BlockSpec((tm, D_MODEL), lambda i: (i, 0)),
        out_shape=jax.ShapeDtypeStruct((t, D_MODEL), F32),
        scratch_shapes=[pltpu.VMEM((tm, D_MODEL), BF16)],
        compiler_params=_cparams(("parallel",)),
        name="mix_ffn",
    )(h, mod, y_fwd, y_bwd, u_t, att, f, d, wglu, bglu, wf, bfn, wo, g, wg, wu, wd, fg)


def _rope_tables(seq):
    rows = seq // GRID_W
    row_id = jnp.repeat(jnp.arange(rows), GRID_W).astype(F32)
    col_id = jnp.tile(jnp.arange(GRID_W), rows).astype(F32)
    n_freq = HEAD_DIM // 4
    inv = ROPE_BASE ** (-jnp.arange(n_freq, dtype=F32) / n_freq)
    ang = jnp.concatenate([row_id[:, None] * inv, col_id[:, None] * inv], axis=-1)
    cos, sin = jnp.cos(ang), jnp.sin(ang)
    cos_full = jnp.tile(jnp.concatenate([cos, cos], axis=-1), (1, N_Q_HEADS))
    sin_signed = jnp.tile(jnp.concatenate([-sin, sin], axis=-1), (1, N_Q_HEADS))
    return cos_full, sin_signed


def _angle_table(mult, n, period):
    m = (mult[:, None] * n[None, :]) % period
    th = m.astype(F32) * (2.0 * math.pi / period)
    return jnp.cos(th), jnp.sin(th)


def _dft_tables(seq):
    n = jnp.arange(seq, dtype=jnp.int32)
    t1c, t1s = _angle_table(DFT_RADIX * jnp.arange(seq // DFT_RADIX, dtype=jnp.int32), n, seq)
    t2c, t2s = _angle_table(jnp.arange(DFT_RADIX, dtype=jnp.int32), n, seq)
    return t1c, t1s, t2c, t2s


def _ssm_params(lam_re, lam_im, b_re, b_im, c_re, c_im, log_dt):
    dt = jnp.exp(log_dt)[..., None]
    mag = jnp.exp(lam_re * dt)
    ar = mag * jnp.cos(lam_im * dt)
    ai = mag * jnp.sin(lam_im * dt)
    den = lam_re * lam_re + lam_im * lam_im
    qr = ((ar - 1.0) * lam_re + ai * lam_im) / den
    qi = (ai * lam_re - (ar - 1.0) * lam_im) / den
    bb_re = qr[..., None] * b_re - qi[..., None] * b_im
    bb_im = qr[..., None] * b_im + qi[..., None] * b_re
    eye = jnp.eye(N_SSM_GROUPS, dtype=F32)

    def block_diag(x):
        y = jnp.transpose(x, (0, 1, 3, 2))[:, :, :, None, :] * eye[None, :, None, :, None]
        return y.reshape(2, x.shape[1] * x.shape[3], x.shape[1] * x.shape[2])

    bmat = jnp.concatenate([block_diag(bb_re), block_diag(bb_im)], axis=-1).astype(BF16)
    cmat = jnp.concatenate([block_diag(c_re), block_diag(-c_im)], axis=1).astype(BF16)
    a2 = jnp.stack([ar.reshape(2, N_STATE), ai.reshape(2, N_STATE)], axis=1)
    return a2, bmat, cmat


def _permute_q_heads(w_in):
    cols = jnp.arange(P_IN, dtype=jnp.int32)
    for base in (256, 768):
        blk = jnp.concatenate([base + h * HEAD_DIM + jnp.arange(HEAD_DIM, dtype=jnp.int32) for h in Q_HEAD_ORDER])
        cols = cols.at[base:base + N_Q_HEADS * HEAD_DIM].set(blk)
    return jnp.take(w_in, cols, axis=1)


def _cache_layout(cache):
    b, _, past, nkv, dh = cache.shape
    keys = cache[:, 0].reshape(b, past, nkv * dh).astype(BF16)
    vt = jnp.transpose(cache[:, 1], (0, 2, 3, 1))
    ones = jnp.ones((b, nkv, VT_ROWS - dh, past), vt.dtype)
    vt = jnp.concatenate([vt, ones], axis=2).reshape(b, nkv * VT_ROWS, past).astype(BF16)
    return keys, vt


def kernel(x_prompt, x_sample, cache_swa_kv, cache_axial_kv, state_ssm, c, c_ctx, w_mod, b_mod, norm_ffn1, norm_mix, norm_ffn2, ffn1_w_gate, ffn1_w_up, ffn1_w_down, ffn2_w_gate, ffn2_w_up, ffn2_w_down, w_in, w_out, ssm_lambda_re, ssm_lambda_im, ssm_b_re, ssm_b_im, ssm_c_re, ssm_c_im, ssm_log_dt, ssm_d, ssm_w_glu, ssm_b_glu, swa_sink, ax_q_norm, ax_k_norm, fnet_w, fnet_b, final_norm):
    depth = w_mod.shape[0]
    cb, cl, _ = x_prompt.shape
    lb, ll, _ = x_sample.shape

    cond = jnp.zeros((16, D_MODEL), F32).at[0].set(c_ctx).at[1:1 + lb].set(c)
    mod_all = _modulation(cond, w_mod, b_mod.reshape(depth, 1, N_MOD * D_MODEL))
    mod_all = mod_all.reshape(depth, 16, N_MOD, D_MODEL)

    e_heads = jnp.kron(jnp.eye(256 // HEAD_DIM, dtype=F32), jnp.ones((HEAD_DIM, HEAD_DIM), F32)) / HEAD_DIM
    e_heads = e_heads.astype(BF16)
    kc = jnp.arange(W_GRP, dtype=jnp.int32)
    cc, sc = _angle_table(kc, kc, W_GRP)
    cs_chan = jnp.concatenate([cc, sc], axis=-1).astype(BF16)
    rope_tabs = _rope_tables(ll)
    dft_ctx = _dft_tables(cl)
    dft_lat = _dft_tables(ll)
    fg = final_norm.reshape(1, D_MODEL)

    h_ctx = x_prompt.reshape(cb * cl, D_MODEL)
    h_lat = x_sample.reshape(lb * ll, D_MODEL)
    tm_lat = 512
    swa_list, ax_list, ssm_list = [], [], []
    ffn1_w = tuple(w.astype(BF16) for w in (ffn1_w_gate, ffn1_w_up, ffn1_w_down))
    ffn2_w = tuple(w.astype(BF16) for w in (ffn2_w_gate, ffn2_w_up, ffn2_w_down))
    for l in range(depth):
        mod = mod_all[l]
        bf = lambda w: w[l].astype(BF16)
        row = lambda v: v[l].reshape(1, -1)
        ffn1 = (row(norm_ffn1), *ffn1_w, fg)
        ffn2 = (row(norm_ffn2), *ffn2_w, fg)
        w_in_l, w_out_l = _permute_q_heads(w_in[l]).astype(BF16), bf(w_out)
        qn = jnp.tile(ax_q_norm[l], N_Q_HEADS).reshape(1, 256)
        kn = jnp.tile(ax_k_norm[l], KV_W // HEAD_DIM).reshape(1, KV_W)
        a2, bmat, cmat = _ssm_params(ssm_lambda_re[l], ssm_lambda_im[l], ssm_b_re[l], ssm_b_im[l],
                                     ssm_c_re[l], ssm_c_im[l], ssm_log_dt[l])
        mix_out_w = (row(ssm_d), bf(ssm_w_glu), row(ssm_b_glu), bf(fnet_w), row(fnet_b), w_out_l)
        sink = swa_sink[l]
        last = l == depth - 1

        h_ctx = _ffn(h_ctx, mod, *ffn1, layer=l, mod_base=0, tm=512, tiles_per_cond=1 << 30, cond_base=0,
                     final_norm=False)
        u_t, qkv, pc, ps, vt, kv_s, kv_a = _mix_in(h_ctx, mod, row(norm_mix), w_in_l, qn, kn, e_heads, cs_chan, None,
                                               batch=cb, seq=cl, tm=cl, cond_base=0, rope=False)
        u3 = u_t.reshape(cl, cb, W_GRP)
        zero_state = jnp.zeros((cb, 2 * N_STATE), F32)
        y_dirs, s_fins = zip(*[_ssm(u3, bmat[d], cmat[d], a2[d], zero_state, batch=cb, seq=cl, lc=128,
                                    reverse=bool(d)) for d in range(2)])
        att = _attn_ctx(sink, qkv, vt, batch=cb, seq=cl)
        f = _dft(dft_ctx, pc, ps, seq=cl, tm=cl, tk=cl, tn=2048)
        h_ctx = _mix_ffn(h_ctx, mod, *y_dirs, u_t, att, f, *mix_out_w, *ffn2, layer=l, batch=cb, seq=cl, tm=cl,
                         cond_base=0, per_batch_cond=False, final_norm=last)
        swa_list.append(kv_s.reshape(cb, 2, cl, KV_W // HEAD_DIM, HEAD_DIM))
        ax_list.append(kv_a.reshape(cb, 2, cl, KV_W // HEAD_DIM, HEAD_DIM))
        s_fin = jnp.stack(s_fins, axis=0).reshape(2, cb, 2, N_SSM_GROUPS, SSM_STATE)
        ssm_list.append(jnp.transpose(s_fin, (1, 0, 3, 4, 2)))

        tpc = ll // tm_lat
        h_lat = _ffn(h_lat, mod, *ffn1, layer=l, mod_base=0, tm=tm_lat, tiles_per_cond=tpc, cond_base=1,
                     final_norm=False)
        u_t, qkv, pc, ps, vt = _mix_in(h_lat, mod, row(norm_mix), w_in_l, qn, kn, e_heads, cs_chan, rope_tabs,
                                       batch=lb, seq=ll, tm=tm_lat, cond_base=1, rope=True)
        s0 = jnp.transpose(state_ssm[:, l], (1, 0, 4, 2, 3)).reshape(2, lb, 2 * N_STATE)
        u3 = u_t.reshape(ll, lb, W_GRP)
        y_dirs = [_ssm(u3, bmat[d], cmat[d], a2[d], s0[d], batch=lb, seq=ll, lc=128, reverse=bool(d))[0]
                  for d in range(2)]
        kc_s, vc_s = _cache_layout(cache_swa_kv[:, l])
        kc_a, vc_a = _cache_layout(cache_axial_kv[:, l])
        att = _attn_lat(sink, qkv, vt, kc_s, vc_s, kc_a, vc_a, batch=lb, seq=ll, tq=256)
        f = _dft(dft_lat, pc, ps, seq=ll, tm=512, tk=2048, tn=1024)
        h_lat = _mix_ffn(h_lat, mod, *y_dirs, u_t, att, f, *mix_out_w, *ffn2, layer=l, batch=lb, seq=ll, tm=tm_lat,
                         cond_base=1, per_batch_cond=True, final_norm=last)

    y_prompt = h_ctx.reshape(cb, cl, D_MODEL)
    y_sample = h_lat.reshape(lb, ll, D_MODEL)
    return (y_prompt, y_sample, jnp.stack(swa_list, axis=1), jnp.stack(ax_list, axis=1),
            jnp.stack(ssm_list, axis=1))
```
